```python
import math
import jax, jax.numpy as jnp
from jax import lax
import numpy as np

D_MODEL = 1024
BATCH = 16
SEQ = 2048
DEPTH = 4

CHUNK = 64
DENSE_Q_BLOCK = 128
SPARSE_Q_BLOCK = 64

HA = 4
DA = 64
HB = 4
DB = 64
BAND_CHUNKS = 8
REL_CLIP = 128
HC = 4
DC = 64
HI = 8
DI = 32
TOPK_MAX = 256

N_ALIBI = HA + HC

COL_SIZES = (HA * 2 * DA, HA * 2 * DA, HA * 2 * DA,
             HB * DB, HB * DB, HB * DB,
             HC * DC, HC * DC, HC * DC,
             HI * DI, DI, HI)
SPLIT_POINTS = tuple(sum(COL_SIZES[:i + 1]) for i in range(len(COL_SIZES) - 1))
D_IN = sum(COL_SIZES)
D_MIX = HA * 2 * DA + HB * DB + HC * DC

N_EXPERTS = 32
TOP_K = 4
D_EXPERT = D_MODEL
SWIGLU_LIMIT = 7.0
SWIGLU_ALPHA = 1.702
EXPERT_BLOCK = 128

DEEPNORM_ALPHA = (2 * DEPTH) ** 0.25
DEEPNORM_BETA = (8 * DEPTH) ** -0.25
EPS = 1e-5
NEG_INF = -1e30

kernel_name = 'hybrid_chunk_causal_diffattn_band_dsa_moe_deepnorm'


def layer_norm(x, g, b):
    xf = x.astype(jnp.float32)
    mu = jnp.mean(xf, axis=-1, keepdims=True)
    var = jnp.mean(jnp.square(xf - mu), axis=-1, keepdims=True)
    return ((xf - mu) * lax.rsqrt(var + EPS) * g.astype(jnp.float32) + b.astype(jnp.float32)).astype(x.dtype)


def rms_norm(x, g):
    xf = x.astype(jnp.float32)
    ms = jnp.mean(jnp.square(xf), axis=-1, keepdims=True)
    return (xf * lax.rsqrt(ms + EPS) * g.astype(jnp.float32)).astype(x.dtype)


def alibi_slopes(n):
    return jnp.exp2(-8.0 * jnp.arange(1, n + 1, dtype=jnp.float32) / n)


def to_blocks(a, blk):
    b, s = a.shape[:2]
    return jnp.moveaxis(a.reshape((b, s // blk, blk) + a.shape[2:]), 1, 0)


def from_blocks(a):
    a = jnp.moveaxis(a, 0, 1)
    return a.reshape((a.shape[0], a.shape[1] * a.shape[2]) + a.shape[3:])


def diff_attention(q, k, v, lam, lam_init, sub_g, slopes):
    bsz, seq = q.shape[:2]
    pos = jnp.arange(seq)
    key_chunk = pos // CHUNK
    scale = DA ** -0.5
    k1, k2 = k[..., 0, :], k[..., 1, :]

    def block(args):
        qb, qpos = args
        s1 = jnp.einsum('bqhd,bshd->bhqs', qb[..., 0, :], k1).astype(jnp.float32) * scale
        s2 = jnp.einsum('bqhd,bshd->bhqs', qb[..., 1, :], k2).astype(jnp.float32) * scale
        dist = jnp.abs(qpos[:, None] - pos[None, :]).astype(jnp.float32)
        allowed = key_chunk[None, :] <= (qpos // CHUNK)[:, None]
        bias = jnp.where(allowed[None], -slopes[:, None, None] * dist[None], NEG_INF)
        p1 = jax.nn.softmax(s1 + bias, axis=-1)
        p2 = jax.nn.softmax(s2 + bias, axis=-1)
        a = (p1 - lam * p2).astype(v.dtype)
        return jnp.einsum('bhqs,bshe->bqhe', a, v)

    qpos_blocks = pos.reshape(-1, DENSE_Q_BLOCK)
    out = from_blocks(lax.map(block, (to_blocks(q, DENSE_Q_BLOCK), qpos_blocks)))
    out = rms_norm(out, sub_g) * (1.0 - lam_init)
    return out.reshape(bsz, seq, HA * 2 * DA)


def band_attention(q, k, v, rel_bias):
    bsz, seq = q.shape[:2]
    nc = seq // CHUNK
    pad = BAND_CHUNKS * CHUNK
    band = (BAND_CHUNKS + 1) * CHUNK
    scale = DB ** -0.5

    def gather_band(a):
        ap = jnp.pad(a, ((0, 0), (pad, 0), (0, 0), (0, 0)))
        ac = ap.reshape(bsz, nc + BAND_CHUNKS, CHUNK, HB, DB)
        return jnp.concatenate([ac[:, j:j + nc] for j in range(BAND_CHUNKS + 1)], axis=2)

    kb, vb = gather_band(k), gather_band(v)
    qc = q.reshape(bsz, nc, CHUNK, HB, DB)
    s = jnp.einsum('bcqhd,bckhd->bhcqk', qc, kb).astype(jnp.float32) * scale
    i = jnp.arange(CHUNK)
    j = jnp.arange(band)
    rel = pad + i[:, None] - j[None, :]
    bias = rel_bias.astype(jnp.float32)[:, jnp.clip(rel, -REL_CLIP, REL_CLIP) + REL_CLIP]
    key_pos = (jnp.arange(nc)[:, None] - BAND_CHUNKS) * CHUNK + j[None, :]
    valid = key_pos >= 0
    s = jnp.where(valid[None, None, :, None, :], s + bias[None, :, None], NEG_INF)
    p = jax.nn.softmax(s, axis=-1).astype(v.dtype)
    out = jnp.einsum('bhcqk,bckhd->bcqhd', p, vb)
    return out.reshape(bsz, seq, HB * DB)


def dsa_attention(q, k, v, q_idx, k_idx, w_idx, slopes):
    bsz, seq = q.shape[:2]
    n_sel = min(TOPK_MAX, seq // 4)
    pos = jnp.arange(seq)
    key_chunk = pos // CHUNK
    scale = DC ** -0.5
    w_scale = (HI ** -0.5) * (DI ** -0.5)
    gather_rows = jax.vmap(lambda a, idx: a[idx])

    def block(args):
        qb, qib, wb, qpos = args
        q_chunk = qpos // CHUNK
        rel = jax.nn.relu(jnp.einsum('bqhd,bsd->bqhs', qib, k_idx).astype(jnp.float32))
        score = jnp.einsum('bqh,bqhs->bqs', wb.astype(jnp.float32) * w_scale, rel)
        allowed = key_chunk[None, :] <= q_chunk[:, None]
        score = jnp.where(allowed[None], score, NEG_INF)
        _, idx = lax.top_k(score, n_sel)
        ks = gather_rows(k, idx)
        vs = gather_rows(v, idx)
        s = jnp.einsum('bqhd,bqkhd->bhqk', qb, ks).astype(jnp.float32) * scale
        dist = jnp.abs(qpos[None, :, None] - idx).astype(jnp.float32)
        valid = (idx // CHUNK) <= q_chunk[None, :, None]
        s = jnp.where(valid[:, None], s - slopes[None, :, None, None] * dist[:, None], NEG_INF)
        p = jax.nn.softmax(s, axis=-1).astype(v.dtype)
        return jnp.einsum('bhqk,bqkhd->bqhd', p, vs)

    qpos_blocks = pos.reshape(-1, SPARSE_Q_BLOCK)
    out = lax.map(block, (to_blocks(q, SPARSE_Q_BLOCK), to_blocks(q_idx, SPARSE_Q_BLOCK),
                          to_blocks(w_idx, SPARSE_Q_BLOCK), qpos_blocks))
    return from_blocks(out).reshape(bsz, seq, HC * DC)


def moe_ffn(x, w_router, b_router, w_gu, b_gu, w_down, b_down):
    bsz, seq, d = x.shape
    n_tok = bsz * seq
    xf = x.reshape(n_tok, d)
    logits = (xf @ w_router).astype(jnp.float32) + b_router.astype(jnp.float32)
    top_val, top_e = lax.top_k(logits, TOP_K)
    gates = jax.nn.softmax(top_val, axis=-1)
    e_flat = top_e.reshape(-1)
    n_asg = n_tok * TOP_K
    order = jnp.argsort(e_flat)
    e_sorted = e_flat[order]
    counts = jnp.bincount(e_flat, length=N_EXPERTS)
    padded = (counts + EXPERT_BLOCK - 1) // EXPERT_BLOCK * EXPERT_BLOCK
    start = jnp.cumsum(counts) - counts
    pend = jnp.cumsum(padded)
    pstart = pend - padded
    rank = jnp.arange(n_asg, dtype=jnp.int32) - start[e_sorted]
    dest = jnp.zeros((n_asg,), jnp.int32).at[order].set(pstart[e_sorted] + rank)
    n_rows = n_asg + N_EXPERTS * EXPERT_BLOCK
    n_blocks = n_rows // EXPERT_BLOCK
    row_token = jnp.full((n_rows,), n_tok, jnp.int32).at[dest].set(
        jnp.arange(n_asg, dtype=jnp.int32) // TOP_K)
    x_pad = jnp.concatenate([xf, jnp.zeros((1, d), xf.dtype)], axis=0)
    x_rows = x_pad[row_token].reshape(n_blocks, EXPERT_BLOCK, d)
    block_expert = jnp.minimum(
        jnp.searchsorted(pend, jnp.arange(n_blocks) * EXPERT_BLOCK, side='right'), N_EXPERTS - 1)

    def expert_block(args):
        xb, e = args
        h = xb @ w_gu[e] + b_gu[e]
        gate = jnp.minimum(h[:, :D_EXPERT], SWIGLU_LIMIT)
        up = jnp.clip(h[:, D_EXPERT:], -SWIGLU_LIMIT, SWIGLU_LIMIT)
        glu = gate * jax.nn.sigmoid(SWIGLU_ALPHA * gate)
        return ((up + 1.0) * glu) @ w_down[e] + b_down[e]

    y_rows = lax.map(expert_block, (x_rows, block_expert)).reshape(n_rows, d)
    y = y_rows[dest].reshape(n_tok, TOP_K, d)
    out = jnp.einsum('tk,tkd->td', gates.astype(y.dtype), y)
    return out.reshape(bsz, seq, d)


def setup_inputs(seed: int = 0) -> dict:
    key = jax.random.key(seed)
    ks = jax.random.split(key, 19)
    f32 = jnp.float32

    def nrm(k, shape, scale):
        return jax.random.normal(k, shape, f32) * scale

    col_scale = np.ones((D_IN,), np.float32)
    for part in (2, 5, 8):
        col_scale[SPLIT_POINTS[part - 1]:SPLIT_POINTS[part]] = DEEPNORM_BETA

    x = nrm(ks[0], (BATCH, SEQ, D_MODEL), 1.0)
    w_in = nrm(ks[1], (DEPTH, D_MODEL, D_IN), D_MODEL ** -0.5) * jnp.asarray(col_scale)
    lam_q1 = nrm(ks[2], (DEPTH, DA), 0.1)
    lam_k1 = nrm(ks[3], (DEPTH, DA), 0.1)
    lam_q2 = nrm(ks[4], (DEPTH, DA), 0.1)
    lam_k2 = nrm(ks[5], (DEPTH, DA), 0.1)
    subln_g = 1.0 + nrm(ks[6], (DEPTH, 2 * DA), 0.02)
    rel_bias = nrm(ks[7], (DEPTH, HB, 2 * REL_CLIP + 1), 0.2)
    w_out = nrm(ks[8], (DEPTH, D_MIX, D_MODEL), D_MIX ** -0.5 * DEEPNORM_BETA)
    ln1_g = 1.0 + nrm(ks[9], (DEPTH, D_MODEL), 0.02)
    ln1_b = nrm(ks[10], (DEPTH, D_MODEL), 0.02)
    w_router = nrm(ks[11], (DEPTH, D_MODEL, N_EXPERTS), D_MODEL ** -0.5)
    b_router = nrm(ks[12], (DEPTH, N_EXPERTS), 0.01)
    w_gu = nrm(ks[13], (DEPTH, N_EXPERTS, D_MODEL, 2 * D_EXPERT), D_MODEL ** -0.5)
    b_gu = nrm(ks[14], (DEPTH, N_EXPERTS, 2 * D_EXPERT), 0.02)
    w_down = nrm(ks[15], (DEPTH, N_EXPERTS, D_EXPERT, D_MODEL), D_EXPERT ** -0.5 * DEEPNORM_BETA)
    b_down = nrm(ks[16], (DEPTH, N_EXPERTS, D_MODEL), 0.02)
    ln2_g = 1.0 + nrm(ks[17], (DEPTH, D_MODEL), 0.02)
    ln2_b = nrm(ks[18], (DEPTH, D_MODEL), 0.02)
    return {'x': x, 'w_in': w_in, 'lam_q1': lam_q1, 'lam_k1': lam_k1, 'lam_q2': lam_q2,
            'lam_k2': lam_k2, 'subln_g': subln_g, 'rel_bias': rel_bias, 'w_out': w_out,
            'ln1_g': ln1_g, 'ln1_b': ln1_b, 'w_router': w_router, 'b_router': b_router,
            'w_gu': w_gu, 'b_gu': b_gu, 'w_down': w_down, 'b_down': b_down,
            'ln2_g': ln2_g, 'ln2_b': ln2_b}


def reference(x, w_in, lam_q1, lam_k1, lam_q2, lam_k2, subln_g, rel_bias, w_out,
              ln1_g, ln1_b, w_router, b_router, w_gu, b_gu, w_down, b_down, ln2_g, ln2_b):
    bsz, seq, _ = x.shape
    slopes = alibi_slopes(N_ALIBI)
    slopes_a, slopes_c = slopes[0::2], slopes[1::2]
    for l in range(DEPTH):
        h = x @ w_in[l]
        qa, ka, va, qb, kb, vb, qc, kc, vc, qi, ki, wi = jnp.split(h, SPLIT_POINTS, axis=-1)
        lam_init = 0.8 - 0.6 * math.exp(-0.3 * l)
        lam = (jnp.exp(jnp.sum(lam_q1[l].astype(jnp.float32) * lam_k1[l].astype(jnp.float32)))
               - jnp.exp(jnp.sum(lam_q2[l].astype(jnp.float32) * lam_k2[l].astype(jnp.float32)))
               + lam_init)
        out_a = diff_attention(qa.reshape(bsz, seq, HA, 2, DA), ka.reshape(bsz, seq, HA, 2, DA),
                               va.reshape(bsz, seq, HA, 2 * DA), lam, lam_init, subln_g[l], slopes_a)
        out_b = band_attention(qb.reshape(bsz, seq, HB, DB), kb.reshape(bsz, seq, HB, DB),
                               vb.reshape(bsz, seq, HB, DB), rel_bias[l])
        out_c = dsa_attention(qc.reshape(bsz, seq, HC, DC), kc.reshape(bsz, seq, HC, DC),
                              vc.reshape(bsz, seq, HC, DC), qi.reshape(bsz, seq, HI, DI), ki, wi, slopes_c)
        mix = jnp.concatenate([out_a, out_b, out_c], axis=-1)
        x = layer_norm(DEEPNORM_ALPHA * x + mix @ w_out[l], ln1_g[l], ln1_b[l])
        ffn = moe_ffn(x, w_router[l], b_router[l], w_gu[l], b_gu[l], w_down[l], b_down[l])
        x = layer_norm(DEEPNORM_ALPHA * x + ffn, ln2_g[l], ln2_b[l])
    return x
```

```python
import functools
import math

import jax
import jax.numpy as jnp
from jax import lax
from jax.experimental import pallas as pl
from jax.experimental.pallas import tpu as pltpu

F32 = jnp.float32
BF16 = jnp.bfloat16
I32 = jnp.int32

D_MODEL = 1024
DEPTH = 4
CHUNK = 64
CHUNK_SHIFT = 6
HA, DA = 4, 64
HB, DB = 4, 64
BAND_CHUNKS = 8
REL_CLIP = 128
HC, DC = 4, 64
HI, DI = 8, 32
TOPK_MAX = 256
N_EXPERTS = 32
TOP_K = 4
D_EXPERT = D_MODEL
SWIGLU_LIMIT = 7.0
SWIGLU_ALPHA = 1.702
DEEPNORM_ALPHA = (2 * DEPTH) ** 0.25
EPS = 1e-5
NEG_INF = -1e30

D_QKV = 3 * HA * 2 * DA + 3 * HB * DB + 3 * HC * DC
D_IK = 2 * HI * DI
D_W = 128
D_MIX = HA * 2 * DA + HB * DB + HC * DC

VMEM_LIMIT = 56 * 1024 * 1024

_NT = (((1,), (1,)), ((), ()))


def _params(sem):
    return pltpu.CompilerParams(dimension_semantics=sem, vmem_limit_bytes=VMEM_LIMIT)


def _in_proj_body(x_ref, w_ref, hm_ref, hik_ref, hw_ref):
    xb = x_ref[...].astype(BF16)
    for n0 in range(0, D_QKV, 512):
        hm_ref[:, n0:n0 + 512] = jnp.dot(
            xb, w_ref[:, n0:n0 + 512], preferred_element_type=F32).astype(BF16)
    hik_ref[...] = jnp.dot(xb, w_ref[:, D_QKV:D_QKV + D_IK], preferred_element_type=F32).astype(BF16)
    hw_ref[...] = jnp.dot(xb, w_ref[:, D_QKV + D_IK:], preferred_element_type=F32)


def _in_proj(x2d, w):
    t = x2d.shape[0]
    tm = 512
    dw = D_QKV + D_IK + D_W
    return pl.pallas_call(
        _in_proj_body,
        grid=(t // tm,),
        in_specs=[pl.BlockSpec((tm, D_MODEL), lambda i: (i, 0)),
                  pl.BlockSpec((D_MODEL, dw), lambda i: (0, 0))],
        out_specs=[pl.BlockSpec((tm, D_QKV), lambda i: (i, 0)),
                   pl.BlockSpec((tm, D_IK), lambda i: (i, 0)),
                   pl.BlockSpec((tm, D_W), lambda i: (i, 0))],
        out_shape=[jax.ShapeDtypeStruct((t, D_QKV), BF16),
                   jax.ShapeDtypeStruct((t, D_IK), BF16),
                   jax.ShapeDtypeStruct((t, D_W), F32)],
        compiler_params=_params(("parallel",)),
    )(x2d, w)


def _diff_attn_body(slope_ref, lam_ref, q_ref, k_ref, v_ref, g_ref, o_ref,
                    m_sc, l_sc, acc_sc, *, tq, tk, out_scale):
    h = pl.program_id(1)
    qi = pl.program_id(2)
    slope = slope_ref[h]
    lam = lam_ref[0]
    q0 = qi * tq
    q = q_ref[0]
    lane = lax.broadcasted_iota(I32, (1, 2 * DA), 1)
    zero = jnp.zeros_like(q)
    q_half = (jnp.where(lane < DA, q, zero), jnp.where(lane >= DA, q, zero))

    m_sc[...] = jnp.full_like(m_sc, NEG_INF)
    l_sc[...] = jnp.zeros_like(l_sc)
    acc_sc[...] = jnp.zeros_like(acc_sc)

    def kv_step(j, carry):
        k0 = pl.multiple_of(j * tk, tk)
        kb = k_ref[0, pl.ds(k0, tk), :]
        vb = v_ref[0, pl.ds(k0, tk), :]
        r = q0 + lax.broadcasted_iota(I32, (tq, tk), 0)
        c = k0 + lax.broadcasted_iota(I32, (tq, tk), 1)
        dist = jnp.abs(r - c).astype(F32)
        allowed = (c >> CHUNK_SHIFT) <= (r >> CHUNK_SHIFT)
        bias = jnp.where(allowed, -slope * dist, NEG_INF)
        for a in range(2):
            s = lax.dot_general(q_half[a], kb, _NT, preferred_element_type=F32) + bias
            m_old = m_sc[a]
            m_new = jnp.maximum(m_old, jnp.max(s, axis=-1, keepdims=True))
            alpha = jnp.exp(m_old - m_new)
            p = jnp.exp(s - m_new)
            l_sc[a] = alpha * l_sc[a] + jnp.sum(p, axis=-1, keepdims=True)
            acc_sc[a] = alpha * acc_sc[a] + jnp.dot(p.astype(BF16), vb, preferred_element_type=F32)
            m_sc[a] = m_new
        return carry

    lax.fori_loop(0, (qi + 1) * (tq // tk), kv_step, 0)

    o = acc_sc[0] / l_sc[0] - lam * (acc_sc[1] / l_sc[1])
    ms = jnp.mean(o * o, axis=-1, keepdims=True)
    o = o * lax.rsqrt(ms + EPS) * g_ref[...] * out_scale
    o_ref[0] = o.astype(BF16)


def _diff_attn(hm3, slopes, lam, sub_g, lam_init):
    b, s, _ = hm3.shape
    tq = tk = 256
    dh = 2 * DA
    body = functools.partial(_diff_attn_body, tq=tq, tk=tk, out_scale=1.0 - lam_init)
    return pl.pallas_call(
        body,
        grid=(b, HA, s // tq),
        in_specs=[pl.BlockSpec(memory_space=pltpu.SMEM),
                  pl.BlockSpec(memory_space=pltpu.SMEM),
                  pl.BlockSpec((1, tq, dh), lambda bi, h, qi: (bi, qi, h)),
                  pl.BlockSpec((1, s, dh), lambda bi, h, qi: (bi, 0, HA + h)),
                  pl.BlockSpec((1, s, dh), lambda bi, h, qi: (bi, 0, 2 * HA + h)),
                  pl.BlockSpec((1, dh), lambda bi, h, qi: (0, 0))],
        out_specs=pl.BlockSpec((1, tq, dh), lambda bi, h, qi: (bi, qi, h)),
        out_shape=jax.ShapeDtypeStruct((b, s, HA * dh), BF16),
        scratch_shapes=[pltpu.VMEM((2, tq, 1), F32), pltpu.VMEM((2, tq, 1), F32),
                        pltpu.VMEM((2, tq, dh), F32)],
        compiler_params=_params(("parallel", "parallel", "arbitrary")),
    )(slopes, lam, hm3, hm3, hm3, sub_g)


BAND_TQ = 256
BAND_TK = 3 * BAND_TQ


def _band_attn_body(q_ref, k0_ref, k1_ref, k2_ref, v0_ref, v1_ref, v2_ref, bias_ref, o_ref):
    i = pl.program_id(1)
    q = q_ref[0]
    k = jnp.concatenate([k0_ref[0], k1_ref[0], k2_ref[0]], axis=0)
    v = jnp.concatenate([v0_ref[0], v1_ref[0], v2_ref[0]], axis=0)
    col = lax.broadcasted_iota(I32, (1, BAND_TK), 1)
    first_valid = jnp.where(i >= 2, 0, jnp.where(i >= 1, BAND_TQ, 2 * BAND_TQ))
    kvalid = col >= first_valid
    lane = lax.broadcasted_iota(I32, (1, HB * DB), 1)
    zero = jnp.zeros_like(q)
    out = jnp.zeros((BAND_TQ, HB * DB), F32)
    for h in range(HB):
        head = (lane >> 6) == h
        s = lax.dot_general(jnp.where(head, q, zero), k, _NT, preferred_element_type=F32)
        s = jnp.where(kvalid, s + bias_ref[h], NEG_INF)
        m = jnp.max(s, axis=-1, keepdims=True)
        p = jnp.exp(s - m)
        l = jnp.sum(p, axis=-1, keepdims=True)
        o = jnp.dot(p.astype(BF16), v, preferred_element_type=F32) / l
        out = jnp.where(head, o, out)
    o_ref[0] = out.astype(BF16)


def _band_bias(rel_bias):
    r = jnp.arange(BAND_TQ)[:, None]
    j = jnp.arange(BAND_TK)[None, :]
    rel = 2 * BAND_TQ + r - j
    qc = r // CHUNK
    kc = j // CHUNK
    in_band = (kc >= qc) & (kc <= qc + BAND_CHUNKS)
    bias = rel_bias.astype(F32)[:, jnp.clip(rel, -REL_CLIP, REL_CLIP) + REL_CLIP]
    return jnp.where(in_band[None], bias, NEG_INF)


def _band_attn(hm3, bias):
    b, s, _ = hm3.shape
    w = HB * DB
    cq, ck, cv = 6, 7, 8

    def kv_spec(col, back):
        return pl.BlockSpec((1, BAND_TQ, w), lambda bi, i: (bi, jnp.maximum(i - back, 0), col))

    return pl.pallas_call(
        _band_attn_body,
        grid=(b, s // BAND_TQ),
        in_specs=[pl.BlockSpec((1, BAND_TQ, w), lambda bi, i: (bi, i, cq)),
                  kv_spec(ck, 2), kv_spec(ck, 1), kv_spec(ck, 0),
                  kv_spec(cv, 2), kv_spec(cv, 1), kv_spec(cv, 0),
                  pl.BlockSpec((HB, BAND_TQ, BAND_TK), lambda bi, i: (0, 0, 0))],
        out_specs=pl.BlockSpec((1, BAND_TQ, w), lambda bi, i: (bi, i, 0)),
        out_shape=jax.ShapeDtypeStruct((b, s, w), BF16),
        compiler_params=_params(("parallel", "parallel")),
    )(hm3, hm3, hm3, hm3, hm3, hm3, hm3, bias)


INT_MIN = -2 ** 31
NEG_INF_KEY = -1900671691
UNSELECTED = -1e33


def _order_key(x):
    bits = lax.bitcast_convert_type(x, I32)
    return jnp.where(bits < 0, bits ^ jnp.int32(0x7FFFFFFF), bits)


def _dsa_body(q_ref, k_ref, v_ref, qi_ref, ki_ref, w_ref, o_ref, key_sc, base_sc,
              *, tq, s_len, n_sel, slopes):
    qi = pl.program_id(1)
    q0 = qi * tq
    r = q0 + lax.broadcasted_iota(I32, (tq, s_len), 0)
    c = lax.broadcasted_iota(I32, (tq, s_len), 1)
    allowed = (c >> CHUNK_SHIFT) <= (r >> CHUNK_SHIFT)

    qidx = qi_ref[0]
    kidx = ki_ref[0]
    w = w_ref[0]
    lane_i = lax.broadcasted_iota(I32, (1, HI * DI), 1)
    zero_i = jnp.zeros_like(qidx)
    score = jnp.zeros((tq, s_len), F32)
    for h in range(HI):
        qm = jnp.where((lane_i >> 5) == h, qidx, zero_i)
        rel = jnp.maximum(lax.dot_general(qm, kidx, _NT, preferred_element_type=F32), 0.0)
        score = score + w[:, h:h + 1] * rel
    score = jnp.where(allowed, score, NEG_INF)

    key_sc[...] = _order_key(score)

    kf = float(n_sel)

    def count_ge(cand):
        return jnp.sum(jnp.where(key_sc[...] >= cand, 1.0, 0.0), axis=-1, keepdims=True)

    thr = jnp.where(count_ge(jnp.zeros((tq, 1), I32)) >= kf, 0, INT_MIN).astype(I32)

    def bit_step(i, thr):
        cand = thr + jnp.left_shift(jnp.int32(1), 30 - i)
        return jnp.where(count_ge(cand) >= kf, cand, thr)

    thr = lax.fori_loop(0, 31, bit_step, thr)

    ge = key_sc[...] >= thr
    n_ge = jnp.sum(jnp.where(ge, 1.0, 0.0), axis=-1, keepdims=True)
    base_sc[...] = jnp.where(ge & allowed, -jnp.abs(r - c).astype(F32), UNSELECTED)

    has_tie = jnp.max(jnp.where((n_ge > kf) & (thr != NEG_INF_KEY), 1.0, 0.0)) > 0.0

    @pl.when(has_tie)
    def _():
        need = kf - jnp.sum(jnp.where(key_sc[...] > thr, 1.0, 0.0), axis=-1, keepdims=True)
        blk = 256
        tri = jnp.where(lax.broadcasted_iota(I32, (blk, blk), 0)
                        < lax.broadcasted_iota(I32, (blk, blk), 1), 1.0, 0.0).astype(BF16)
        rb = q0 + lax.broadcasted_iota(I32, (tq, blk), 0)
        offs = jnp.zeros((tq, 1), F32)
        for jb in range(s_len // blk):
            sl = slice(jb * blk, (jb + 1) * blk)
            kb = key_sc[:, sl]
            cb = jb * blk + lax.broadcasted_iota(I32, (tq, blk), 1)
            e = jnp.where(kb == thr, 1.0, 0.0)
            rank = jnp.dot(e.astype(BF16), tri, preferred_element_type=F32) + offs
            keep = jnp.where(kb > thr, 1.0, jnp.where(rank < need, e, 0.0))
            keep = jnp.where((cb >> CHUNK_SHIFT) <= (rb >> CHUNK_SHIFT), keep, 0.0)
            base_sc[:, sl] = jnp.where(keep > 0.0, -jnp.abs(rb - cb).astype(F32), UNSELECTED)
            offs = offs + jnp.sum(e, axis=-1, keepdims=True)

    q = q_ref[0]
    k = k_ref[0]
    v = v_ref[0]
    lane = lax.broadcasted_iota(I32, (1, HC * DC), 1)
    zero = jnp.zeros_like(q)
    out = jnp.zeros((tq, HC * DC), F32)
    for h in range(HC):
        head = (lane >> 6) == h
        s = lax.dot_general(jnp.where(head, q, zero), k, _NT, preferred_element_type=F32)
        s = s + slopes[h] * base_sc[...]
        m = jnp.max(s, axis=-1, keepdims=True)
        p = jnp.exp(s - m)
        l = jnp.sum(p, axis=-1, keepdims=True)
        o = jnp.dot(p.astype(BF16), v, preferred_element_type=F32) / l
        out = jnp.where(head, o, out)
    o_ref[0] = out.astype(BF16)


def _dsa_attn(hm3, hik3, hw3, slopes):
    b, s, _ = hm3.shape
    tq = 128
    w = HC * DC
    cq, ck, cv = 9, 10, 11
    body = functools.partial(_dsa_body, tq=tq, s_len=s, n_sel=min(TOPK_MAX, s // 4), slopes=slopes)
    return pl.pallas_call(
        body,
        grid=(b, s // tq),
        in_specs=[pl.BlockSpec((1, tq, w), lambda bi, i: (bi, i, cq)),
                  pl.BlockSpec((1, s, w), lambda bi, i: (bi, 0, ck)),
                  pl.BlockSpec((1, s, w), lambda bi, i: (bi, 0, cv)),
                  pl.BlockSpec((1, tq, HI * DI), lambda bi, i: (bi, i, 0)),
                  pl.BlockSpec((1, s, HI * DI), lambda bi, i: (bi, 0, 1)),
                  pl.BlockSpec((1, tq, D_W), lambda bi, i: (bi, i, 0))],
        out_specs=pl.BlockSpec((1, tq, w), lambda bi, i: (bi, i, 0)),
        out_shape=jax.ShapeDtypeStruct((b, s, w), BF16),
        scratch_shapes=[pltpu.VMEM((tq, s), I32), pltpu.VMEM((tq, s), F32)],
        compiler_params=_params(("parallel", "arbitrary")),
    )(hm3, hm3, hm3, hik3, hik3, hw3)


def _layer_norm(y, g, b):
    mu = jnp.mean(y, axis=-1, keepdims=True)
    d = y - mu
    var = jnp.mean(d * d, axis=-1, keepdims=True)
    return d * lax.rsqrt(var + EPS) * g + b


def _out_router_body(a_ref, b_ref, c_ref, x_ref, wo_ref, g_ref, beta_ref, wr_ref, br_ref,
                     x1_ref, x1b_ref, meta_ref, cnt_ref, *, tm):
    i = pl.program_id(0)
    wa = HA * 2 * DA
    wb = wa + HB * DB
    y = (jnp.dot(a_ref[...], wo_ref[0:wa, :], preferred_element_type=F32)
         + jnp.dot(b_ref[...], wo_ref[wa:wb, :], preferred_element_type=F32)
         + jnp.dot(c_ref[...], wo_ref[wb:, :], preferred_element_type=F32))
    x1 = _layer_norm(DEEPNORM_ALPHA * x_ref[...] + y, g_ref[...], beta_ref[...])
    x1_ref[...] = x1
    x1b_ref[...] = x1.astype(BF16)

    logits = jnp.dot(x1, wr_ref[...], preferred_element_type=F32,
                     precision=lax.Precision.HIGHEST) + br_ref[...]
    lane_e = lax.broadcasted_iota(I32, (tm, N_EXPERTS), 1).astype(F32)
    sel = jnp.zeros((tm, N_EXPERTS), F32)
    work = logits
    picks, vals = [], []
    for _ in range(TOP_K):
        v = jnp.max(work, axis=-1, keepdims=True)
        e = jnp.min(jnp.where(work == v, lane_e, float(N_EXPERTS)), axis=-1, keepdims=True)
        hit = lane_e == e
        sel = jnp.where(hit, 1.0, sel)
        work = jnp.where(hit, -jnp.inf, work)
        picks.append(e)
        vals.append(v)
    ex = [jnp.exp(v - vals[0]) for v in vals]
    den = ex[0] + ex[1] + ex[2] + ex[3]

    @pl.when(i == 0)
    def _():
        cnt_ref[...] = jnp.zeros_like(cnt_ref)

    tri = (lax.broadcasted_iota(I32, (tm, tm), 1) < lax.broadcasted_iota(I32, (tm, tm), 0)).astype(BF16)
    before = jnp.dot(tri, sel.astype(BF16), preferred_element_type=F32) + cnt_ref[0:1, 0:N_EXPERTS]
    cnt_ref[0:1, 0:N_EXPERTS] = cnt_ref[0:1, 0:N_EXPERTS] + jnp.sum(sel, axis=0, keepdims=True)

    lane = lax.broadcasted_iota(I32, (tm, 128), 1)
    meta = jnp.zeros((tm, 128), F32)
    for kk in range(TOP_K):
        rank = jnp.sum(jnp.where(lane_e == picks[kk], before, 0.0), axis=-1, keepdims=True)
        meta = jnp.where(lane == kk, picks[kk], meta)
        meta = jnp.where(lane == TOP_K + kk, rank, meta)
        meta = jnp.where(lane == 2 * TOP_K + kk, ex[kk] / den, meta)
    meta_ref[...] = meta


def _out_router(oa, ob, oc, x2d, wo, g, beta, wr, br):
    t = x2d.shape[0]
    tm = 512
    row = lambda i: (i, 0)
    fix = lambda i: (0, 0)
    body = functools.partial(_out_router_body, tm=tm)
    return pl.pallas_call(
        body,
        grid=(t // tm,),
        in_specs=[pl.BlockSpec((tm, oa.shape[1]), row), pl.BlockSpec((tm, ob.shape[1]), row),
                  pl.BlockSpec((tm, oc.shape[1]), row), pl.BlockSpec((tm, D_MODEL), row),
                  pl.BlockSpec((D_MIX, D_MODEL), fix), pl.BlockSpec((1, D_MODEL), fix),
                  pl.BlockSpec((1, D_MODEL), fix), pl.BlockSpec((D_MODEL, N_EXPERTS), fix),
                  pl.BlockSpec((1, N_EXPERTS), fix)],
        out_specs=[pl.BlockSpec((tm, D_MODEL), row), pl.BlockSpec((tm, D_MODEL), row),
                   pl.BlockSpec((tm, 128), row), pl.BlockSpec((8, 128), fix)],
        out_shape=[jax.ShapeDtypeStruct((t, D_MODEL), F32), jax.ShapeDtypeStruct((t, D_MODEL), BF16),
                   jax.ShapeDtypeStruct((t, 128), F32), jax.ShapeDtypeStruct((8, 128), F32)],
        compiler_params=_params(("arbitrary",)),
    )(oa, ob, oc, x2d, wo, g, beta, wr, br)


MOE_ROWS = 256


def _moe_body(be_ref, bv_ref, x_ref, wgu_ref, bgu_ref, wd_ref, bd_ref, y_ref):
    i = pl.program_id(0)

    @pl.when(bv_ref[i] > 0)
    def _():
        h = jnp.dot(x_ref[...], wgu_ref[0], preferred_element_type=F32) + bgu_ref[0]
        gate = jnp.minimum(h[:, :D_EXPERT], SWIGLU_LIMIT)
        up = jnp.clip(h[:, D_EXPERT:], -SWIGLU_LIMIT, SWIGLU_LIMIT)
        glu = gate * (1.0 / (1.0 + jnp.exp(-SWIGLU_ALPHA * gate)))
        act = ((up + 1.0) * glu).astype(BF16)
        y_ref[...] = jnp.dot(act, wd_ref[0], preferred_element_type=F32) + bd_ref[0]

    @pl.when(bv_ref[i] == 0)
    def _():
        y_ref[...] = jnp.zeros_like(y_ref)


def _moe(block_expert, block_valid, x_rows, wgu, bgu, wd, bd):
    n_rows = x_rows.shape[0]
    grid_spec = pltpu.PrefetchScalarGridSpec(
        num_scalar_prefetch=2,
        grid=(n_rows // MOE_ROWS,),
        in_specs=[pl.BlockSpec((MOE_ROWS, D_MODEL), lambda i, be, bv: (i, 0)),
                  pl.BlockSpec((1, D_MODEL, 2 * D_EXPERT), lambda i, be, bv: (be[i], 0, 0)),
                  pl.BlockSpec((1, 1, 2 * D_EXPERT), lambda i, be, bv: (be[i], 0, 0)),
                  pl.BlockSpec((1, D_EXPERT, D_MODEL), lambda i, be, bv: (be[i], 0, 0)),
                  pl.BlockSpec((1, 1, D_MODEL), lambda i, be, bv: (be[i], 0, 0))],
        out_specs=pl.BlockSpec((MOE_ROWS, D_MODEL), lambda i, be, bv: (i, 0)),
    )
    return pl.pallas_call(
        _moe_body,
        grid_spec=grid_spec,
        out_shape=jax.ShapeDtypeStruct((n_rows, D_MODEL), F32),
        compiler_params=_params(("arbitrary",)),
    )(block_expert, block_valid, x_rows, wgu, bgu, wd, bd)


def _combine_body(x_ref, y0_ref, y1_ref, y2_ref, y3_ref, meta_ref, g_ref, beta_ref, o_ref):
    gates = meta_ref[...]
    ffn = (gates[:, 8:9] * y0_ref[...] + gates[:, 9:10] * y1_ref[...]
           + gates[:, 10:11] * y2_ref[...] + gates[:, 11:12] * y3_ref[...])
    o_ref[...] = _layer_norm(DEEPNORM_ALPHA * x_ref[...] + ffn, g_ref[...], beta_ref[...])


def _combine(x1, ys, meta, g, beta):
    t = x1.shape[0]
    tm = 512
    row = lambda i: (i, 0)
    fix = lambda i: (0, 0)
    return pl.pallas_call(
        _combine_body,
        grid=(t // tm,),
        in_specs=[pl.BlockSpec((tm, D_MODEL), row)] * 5
                 + [pl.BlockSpec((tm, 128), row), pl.BlockSpec((1, D_MODEL), fix),
                    pl.BlockSpec((1, D_MODEL), fix)],
        out_specs=pl.BlockSpec((tm, D_MODEL), row),
        out_shape=jax.ShapeDtypeStruct((t, D_MODEL), F32),
        compiler_params=_params(("parallel",)),
    )(x1, *ys, meta, g, beta)


def _prep_w_in(w):
    scale = DA ** -0.5
    col = jnp.ones((D_QKV,), F32)
    for lo, width in ((0, HA * 2 * DA), (3 * HA * 2 * DA, HB * DB),
                      (3 * HA * 2 * DA + 3 * HB * DB, HC * DC)):
        col = col.at[lo:lo + width].set(scale)
    main = w[:, :D_QKV] * col
    q_idx = w[:, D_QKV:D_QKV + HI * DI]
    k_idx = w[:, D_QKV + HI * DI:D_QKV + HI * DI + DI]
    w_idx = w[:, D_QKV + HI * DI + DI:] * ((HI ** -0.5) * (DI ** -0.5))
    w_pad = jnp.zeros((D_MODEL, D_W - HI), F32)
    return jnp.concatenate([main, q_idx, jnp.tile(k_idx, (1, HI)), w_idx, w_pad], axis=1).astype(BF16)


def _route(meta, cnt, n_tok):
    top_e = meta[:, 0:TOP_K].astype(I32)
    rank = meta[:, TOP_K:2 * TOP_K].astype(I32)
    counts = cnt[0, :N_EXPERTS].astype(I32)
    n_blocks = n_tok * TOP_K // MOE_ROWS + N_EXPERTS
    blocks_e = (counts + MOE_ROWS - 1) // MOE_ROWS
    blocks_end = jnp.cumsum(blocks_e)
    row_start = (blocks_end - blocks_e) * MOE_ROWS
    dest = row_start[top_e] + rank
    tok = jnp.broadcast_to(jnp.arange(n_tok, dtype=I32)[:, None], (n_tok, TOP_K))
    row_token = jnp.zeros((n_blocks * MOE_ROWS,), I32).at[dest.reshape(-1)].set(
        tok.reshape(-1), unique_indices=True)
    blk = jnp.arange(n_blocks, dtype=I32)
    block_expert = jnp.minimum(jnp.searchsorted(blocks_end, blk, side='right'), N_EXPERTS - 1).astype(I32)
    block_valid = (blk < blocks_end[-1]).astype(I32)
    return dest, row_token, block_expert, block_valid


def kernel(x, w_in, lam_q1, lam_k1, lam_q2, lam_k2, subln_g, rel_bias, w_out, ln1_g, ln1_b,
           w_router, b_router, w_gu, b_gu, w_down, b_down, ln2_g, ln2_b):
    bsz, seq, _ = x.shape
    n_tok = bsz * seq
    slopes = [2.0 ** (-8.0 * i / (HA + HC)) for i in range(1, HA + HC + 1)]
    slopes_a = jnp.asarray(slopes[0::2], F32)
    slopes_c = tuple(slopes[1::2])
    x2d = x.reshape(n_tok, D_MODEL)
    for l in range(DEPTH):
        lam_init = 0.8 - 0.6 * math.exp(-0.3 * l)
        lam = (jnp.exp(jnp.sum(lam_q1[l] * lam_k1[l])) - jnp.exp(jnp.sum(lam_q2[l] * lam_k2[l]))
               + lam_init).reshape(1).astype(F32)
        hm, hik, hw = _in_proj(x2d, _prep_w_in(w_in[l]))
        hm3 = hm.reshape(bsz, seq, D_QKV)
        out_a = _diff_attn(hm3, slopes_a, lam, subln_g[l].reshape(1, 2 * DA), lam_init)
        out_b = _band_attn(hm3, _band_bias(rel_bias[l]))
        out_c = _dsa_attn(hm3, hik.reshape(bsz, seq, D_IK), hw.reshape(bsz, seq, D_W), slopes_c)
        x1, x1b, meta, cnt = _out_router(
            out_a.reshape(n_tok, -1), out_b.reshape(n_tok, -1), out_c.reshape(n_tok, -1), x2d,
            w_out[l].astype(BF16), ln1_g[l].reshape(1, -1), ln1_b[l].reshape(1, -1),
            w_router[l], b_router[l].reshape(1, -1))
        dest, row_token, block_expert, block_valid = _route(meta, cnt, n_tok)
        x_rows = x1b[row_token]
        y_rows = _moe(block_expert, block_valid, x_rows, w_gu[l].astype(BF16),
                      b_gu[l].reshape(N_EXPERTS, 1, -1), w_down[l].astype(BF16),
                      b_down[l].reshape(N_EXPERTS, 1, -1))
        ys = [y_rows[dest[:, kk]] for kk in range(TOP_K)]
        x2d = _combine(x1, ys, meta, ln2_g[l].reshape(1, -1), ln2_b[l].reshape(1, -1))
    return x2d.reshape(bsz, seq, D_MODEL)
```

```python
import functools
import math

import jax
import jax.numpy as jnp
from jax import lax
from jax.experimental import pallas as pl
from jax.experimental.pallas import tpu as pltpu

F32 = jnp.float32
BF16 = jnp.bfloat16
I32 = jnp.int32

D_MODEL = 1024
DEPTH = 4
CHUNK = 64
CHUNK_SHIFT = 6
HA, DA = 4, 64
HB, DB = 4, 64
BAND_CHUNKS = 8
REL_CLIP = 128
HC, DC = 4, 64
HI, DI = 8, 32
TOPK_MAX = 256
N_EXPERTS = 32
TOP_K = 4
D_EXPERT = D_MODEL
SWIGLU_LIMIT = 7.0
SWIGLU_ALPHA = 1.702
DEEPNORM_ALPHA = (2 * DEPTH) ** 0.25
EPS = 1e-5
NEG_INF = -1e30

D_QKV = 3 * HA * 2 * DA + 3 * HB * DB + 3 * HC * DC
D_IK = 2 * HI * DI
D_W = 128
D_MIX = HA * 2 * DA + HB * DB + HC * DC

VMEM_LIMIT = 56 * 1024 * 1024

_NT = (((1,), (1,)), ((), ()))
_TN = (((0,), (0,)), ((), ()))


def _params(sem):
    return pltpu.CompilerParams(dimension_semantics=sem, vmem_limit_bytes=VMEM_LIMIT)


def _col_sum32(x):
    n, t = x.shape
    return x.reshape(n // 32, 32, t).sum(axis=0)


def _in_proj_body(x_ref, w_ref, hm_ref, hik_ref, hw_ref):
    xb = x_ref[...].astype(BF16)
    for n0 in range(0, D_QKV, 512):
        hm_ref[:, n0:n0 + 512] = jnp.dot(
            xb, w_ref[:, n0:n0 + 512], preferred_element_type=F32).astype(BF16)
    hik_ref[...] = jnp.dot(xb, w_ref[:, D_QKV:D_QKV + D_IK], preferred_element_type=F32).astype(BF16)
    hw_ref[...] = jnp.dot(xb, w_ref[:, D_QKV + D_IK:], preferred_element_type=F32)


def _in_proj(x2d, w):
    t = x2d.shape[0]
    tm = 512
    dw = D_QKV + D_IK + D_W
    return pl.pallas_call(
        _in_proj_body,
        grid=(t // tm,),
        in_specs=[pl.BlockSpec((tm, D_MODEL), lambda i: (i, 0)),
                  pl.BlockSpec((D_MODEL, dw), lambda i: (0, 0))],
        out_specs=[pl.BlockSpec((tm, D_QKV), lambda i: (i, 0)),
                   pl.BlockSpec((tm, D_IK), lambda i: (i, 0)),
                   pl.BlockSpec((tm, D_W), lambda i: (i, 0))],
        out_shape=[jax.ShapeDtypeStruct((t, D_QKV), BF16),
                   jax.ShapeDtypeStruct((t, D_IK), BF16),
                   jax.ShapeDtypeStruct((t, D_W), F32)],
        compiler_params=_params(("parallel",)),
        name="in_proj",
    )(x2d, w)


def _diff_attn_body(lam_ref, q_ref, k_ref, v_ref, g_ref, o_ref,
                    bias_sc, z_sc, p_sc, *state, t, slopes, out_scale):
    qi = pl.program_id(1)
    lam = lam_ref[0]
    dh = 2 * DA
    lane = lax.broadcasted_iota(I32, (1, dh), 1)
    q_half = []
    for h in range(HA):
        qh = q_ref[0, :, h * dh:(h + 1) * dh]
        zero = jnp.zeros_like(qh)
        q_half.append((jnp.where(lane < DA, qh, zero), jnp.where(lane >= DA, qh, zero)))

    kk = lax.broadcasted_iota(I32, (t, t), 0)
    tt = lax.broadcasted_iota(I32, (t, t), 1)
    own_chunk = (kk >> CHUNK_SHIFT) <= (tt >> CHUNK_SHIFT)
    for h in range(HA):
        bias_sc[h] = slopes[h] * kk.astype(F32)
        bias_sc[HA + h] = jnp.where(own_chunk, slopes[h] * jnp.minimum(kk, 2 * tt - kk).astype(F32), NEG_INF)

    n_chain = 2 * HA
    m_sc, l_sc, acc_sc = state[:n_chain], state[n_chain:2 * n_chain], state[2 * n_chain:]
    for c in range(n_chain):
        m_sc[c][...] = jnp.full((1, t), NEG_INF, F32)
        l_sc[c][...] = jnp.zeros((1, t), F32)
        acc_sc[c][...] = jnp.zeros((dh, t), F32)

    def kv_block(j, bias_base):
        k0 = pl.multiple_of(j * t, t)
        k0f = k0.astype(F32)
        zmax = []
        for h in range(HA):
            kb = k_ref[0, pl.ds(k0, t), h * dh:(h + 1) * dh]
            bias = bias_sc[bias_base + h]
            for a in range(2):
                z = lax.dot_general(kb, q_half[h][a], _NT, preferred_element_type=F32) + bias
                z_sc[2 * h + a] = z
                zmax.append(jnp.max(z, axis=0, keepdims=True))
        alphas = []
        for c in range(n_chain):
            shift = slopes[c // 2] * k0f
            m_old = m_sc[c][...]
            m_new = jnp.maximum(m_old, zmax[c] + shift)
            alpha = jnp.exp(m_old - m_new)
            p = jnp.exp(z_sc[c] - (m_new - shift))
            l_sc[c][...] = alpha * l_sc[c][...] + jnp.sum(p, axis=0, keepdims=True)
            m_sc[c][...] = m_new
            p_sc[c] = p.astype(BF16)
            alphas.append(alpha)
        for h in range(HA):
            vt = v_ref[0, pl.ds(k0, t), h * dh:(h + 1) * dh].T
            for a in range(2):
                c = 2 * h + a
                acc_sc[c][...] = alphas[c] * acc_sc[c][...] + jnp.dot(
                    vt, p_sc[c], preferred_element_type=F32)

    def past_step(j, carry):
        kv_block(j, 0)
        return carry

    lax.fori_loop(0, qi, past_step, 0)
    kv_block(qi, HA)

    outs = []
    for h in range(HA):
        ot = (acc_sc[2 * h][...] / l_sc[2 * h][...]
              - lam * (acc_sc[2 * h + 1][...] / l_sc[2 * h + 1][...]))
        ms = jnp.mean(ot * ot, axis=0, keepdims=True)
        outs.append((ot * lax.rsqrt(ms + EPS)).T * g_ref[...] * out_scale)
    o_ref[0] = jnp.concatenate(outs, axis=1).astype(BF16)


def _diff_attn(hm3, slopes, lam, sub_g, lam_init):
    b, s, _ = hm3.shape
    t = 256
    dh = 2 * DA
    w = HA * dh
    body = functools.partial(_diff_attn_body, t=t, slopes=slopes, out_scale=1.0 - lam_init)
    return pl.pallas_call(
        body,
        grid=(b, s // t),
        in_specs=[pl.BlockSpec(memory_space=pltpu.SMEM),
                  pl.BlockSpec((1, t, w), lambda bi, qi: (bi, qi, 0)),
                  pl.BlockSpec((1, s, w), lambda bi, qi: (bi, 0, 1)),
                  pl.BlockSpec((1, s, w), lambda bi, qi: (bi, 0, 2)),
                  pl.BlockSpec((1, dh), lambda bi, qi: (0, 0))],
        out_specs=pl.BlockSpec((1, t, w), lambda bi, qi: (bi, qi, 0)),
        out_shape=jax.ShapeDtypeStruct((b, s, w), BF16),
        scratch_shapes=([pltpu.VMEM((2 * HA, t, t), F32), pltpu.VMEM((2 * HA, t, t), F32),
                         pltpu.VMEM((2 * HA, t, t), BF16)]
                        + [pltpu.VMEM((1, t), F32)] * (4 * HA) + [pltpu.VMEM((dh, t), F32)] * (2 * HA)),
        compiler_params=_params(("parallel", "arbitrary")),
        name="diff_attn",
    )(lam, hm3, hm3, hm3, sub_g)


BAND_TQ = 256
BAND_TK = 3 * BAND_TQ


def _band_attn_body(q_ref, k0_ref, k1_ref, k2_ref, v0_ref, v1_ref, v2_ref, bias_ref, o_ref):
    i = pl.program_id(1)
    q = q_ref[0]
    k = jnp.concatenate([k0_ref[0], k1_ref[0], k2_ref[0]], axis=0)
    v = jnp.concatenate([v0_ref[0], v1_ref[0], v2_ref[0]], axis=0)
    col = lax.broadcasted_iota(I32, (1, BAND_TK), 1)
    first_valid = jnp.where(i >= 2, 0, jnp.where(i >= 1, BAND_TQ, 2 * BAND_TQ))
    kvalid = col >= first_valid
    lane = lax.broadcasted_iota(I32, (1, HB * DB), 1)
    zero = jnp.zeros_like(q)
    out = jnp.zeros((BAND_TQ, HB * DB), F32)
    for h in range(HB):
        head = (lane >> 6) == h
        s = lax.dot_general(jnp.where(head, q, zero), k, _NT, preferred_element_type=F32)
        s = jnp.where(kvalid, s + bias_ref[h], NEG_INF)
        m = jnp.max(s, axis=-1, keepdims=True)
        p = jnp.exp(s - m)
        l = jnp.sum(p, axis=-1, keepdims=True)
        o = jnp.dot(p.astype(BF16), v, preferred_element_type=F32) / l
        out = jnp.where(head, o, out)
    o_ref[0] = out.astype(BF16)


def _band_bias(rel_bias):
    r = jnp.arange(BAND_TQ)[:, None]
    j = jnp.arange(BAND_TK)[None, :]
    rel = 2 * BAND_TQ + r - j
    qc = r // CHUNK
    kc = j // CHUNK
    in_band = (kc >= qc) & (kc <= qc + BAND_CHUNKS)
    bias = rel_bias.astype(F32)[:, jnp.clip(rel, -REL_CLIP, REL_CLIP) + REL_CLIP]
    return jnp.where(in_band[None], bias, NEG_INF)


def _band_attn(hm3, bias):
    b, s, _ = hm3.shape
    w = HB * DB
    cq, ck, cv = 6, 7, 8

    def kv_spec(col, back):
        return pl.BlockSpec((1, BAND_TQ, w), lambda bi, i: (bi, jnp.maximum(i - back, 0), col))

    return pl.pallas_call(
        _band_attn_body,
        grid=(b, s // BAND_TQ),
        in_specs=[pl.BlockSpec((1, BAND_TQ, w), lambda bi, i: (bi, i, cq)),
                  kv_spec(ck, 2), kv_spec(ck, 1), kv_spec(ck, 0),
                  kv_spec(cv, 2), kv_spec(cv, 1), kv_spec(cv, 0),
                  pl.BlockSpec((HB, BAND_TQ, BAND_TK), lambda bi, i: (0, 0, 0))],
        out_specs=pl.BlockSpec((1, BAND_TQ, w), lambda bi, i: (bi, i, 0)),
        out_shape=jax.ShapeDtypeStruct((b, s, w), BF16),
        compiler_params=_params(("parallel", "parallel")),
        name="band_attn",
    )(hm3, hm3, hm3, hm3, hm3, hm3, hm3, bias)


INT_MIN = -2 ** 31
NEG_INF_KEY = -1900671691
UNSELECTED = -1e33
DSA_T = 256


def _order_key(x):
    bits = lax.bitcast_convert_type(x, I32)
    return jnp.where(bits < 0, bits ^ jnp.int32(0x7FFFFFFF), bits)


def _dsa_body(q_ref, k_ref, v_ref, qi_ref, ki_ref, w_ref, o_ref, key_sc, base_sc, z_sc, p_sc, *state,
              n_sel, slopes):
    t = DSA_T
    qi = pl.program_id(1)
    q0 = qi * t
    nblk = qi + 1
    kk = lax.broadcasted_iota(I32, (t, t), 0)
    tg = q0 + lax.broadcasted_iota(I32, (t, t), 1)

    def block_start(j):
        return pl.multiple_of(j * t, t)

    qidx = qi_ref[0]
    lane_i = lax.broadcasted_iota(I32, (1, HI * DI), 1)
    zero_i = jnp.zeros_like(qidx)
    qm = [jnp.where((lane_i >> 5) == h, qidx, zero_i) for h in range(HI)]
    wt = w_ref[0].T

    def score_step(j, carry):
        k0 = block_start(j)
        kib = ki_ref[0, pl.ds(k0, t), :]
        sc = jnp.zeros((t, t), F32)
        for h in range(HI):
            rel = jnp.maximum(lax.dot_general(kib, qm[h], _NT, preferred_element_type=F32), 0.0)
            sc = sc + wt[h:h + 1, :] * rel
        allowed = ((k0 + kk) >> CHUNK_SHIFT) <= (tg >> CHUNK_SHIFT)
        key_sc[pl.ds(k0, t), :] = _order_key(jnp.where(allowed, sc, NEG_INF))
        return carry

    lax.fori_loop(0, nblk, score_step, 0)

    kf = float(n_sel)

    def count_ge(cand):
        def blk(j, acc):
            kb = key_sc[pl.ds(block_start(j), t), :]
            return acc + _col_sum32(jnp.where(kb >= cand, 1.0, 0.0))
        acc = lax.fori_loop(0, nblk, blk, jnp.zeros((32, t), F32))
        return jnp.sum(acc, axis=0, keepdims=True)

    thr = jnp.where(count_ge(jnp.zeros((1, t), I32)) >= kf, 0, INT_MIN).astype(I32)

    def bit_step(i, thr):
        cand = thr + jnp.left_shift(jnp.int32(1), 30 - i)
        return jnp.where(count_ge(cand) >= kf, cand, thr)

    thr = lax.fori_loop(0, 31, bit_step, thr)

    def alibi_or_unselected(keep, k0):
        kg = k0 + kk
        allowed = (kg >> CHUNK_SHIFT) <= (tg >> CHUNK_SHIFT)
        return jnp.where(keep, jnp.where(allowed, -jnp.abs(tg - kg).astype(F32), UNSELECTED), UNSELECTED)

    def select_step(j, n_ge):
        k0 = block_start(j)
        ge = key_sc[pl.ds(k0, t), :] >= thr
        base_sc[pl.ds(k0, t), :] = alibi_or_unselected(ge, k0)
        return n_ge + _col_sum32(jnp.where(ge, 1.0, 0.0))

    n_ge = jnp.sum(lax.fori_loop(0, nblk, select_step, jnp.zeros((32, t), F32)), axis=0, keepdims=True)
    over = jnp.where(n_ge > kf, jnp.where(thr != NEG_INF_KEY, 1.0, 0.0), 0.0)

    @pl.when(jnp.max(over) > 0.0)
    def _():
        def gt_step(j, acc):
            kb = key_sc[pl.ds(block_start(j), t), :]
            return acc + _col_sum32(jnp.where(kb > thr, 1.0, 0.0))
        n_gt = jnp.sum(lax.fori_loop(0, nblk, gt_step, jnp.zeros((32, t), F32)), axis=0, keepdims=True)
        need = kf - n_gt
        earlier = jnp.where(lax.broadcasted_iota(I32, (t, t), 1) < kk, 1.0, 0.0).astype(BF16)

        def tie_step(j, offs):
            k0 = block_start(j)
            kb = key_sc[pl.ds(k0, t), :]
            e = jnp.where(kb == thr, 1.0, 0.0)
            rank = jnp.dot(earlier, e.astype(BF16), preferred_element_type=F32) + offs
            keep = jnp.where(kb > thr, 1.0, jnp.where(rank < need, e, 0.0))
            base_sc[pl.ds(k0, t), :] = alibi_or_unselected(keep > 0.0, k0)
            return offs + jnp.sum(e, axis=0, keepdims=True)

        lax.fori_loop(0, nblk, tie_step, jnp.zeros((1, t), F32))

    q = q_ref[0]
    lane = lax.broadcasted_iota(I32, (1, HC * DC), 1)
    zero = jnp.zeros_like(q)
    q_head = [jnp.where((lane >> 6) == h, q, zero) for h in range(HC)]
    m_sc, l_sc, acc_sc = state[:HC], state[HC:2 * HC], state[2 * HC:]
    for h in range(HC):
        m_sc[h][...] = jnp.full((1, t), NEG_INF, F32)
        l_sc[h][...] = jnp.zeros((1, t), F32)
        acc_sc[h][...] = jnp.zeros((DC, t), F32)

    def att_step(j, carry):
        k0 = block_start(j)
        kb = k_ref[0, pl.ds(k0, t), :]
        base = base_sc[pl.ds(k0, t), :]
        zmax = []
        for h in range(HC):
            z = lax.dot_general(kb, q_head[h], _NT, preferred_element_type=F32) + slopes[h] * base
            z_sc[h] = z
            zmax.append(jnp.max(z, axis=0, keepdims=True))
        alphas = []
        for h in range(HC):
            m_old = m_sc[h][...]
            m_new = jnp.maximum(m_old, zmax[h])
            alpha = jnp.exp(m_old - m_new)
            p = jnp.exp(z_sc[h] - m_new)
            l_sc[h][...] = alpha * l_sc[h][...] + jnp.sum(p, axis=0, keepdims=True)
            m_sc[h][...] = m_new
            p_sc[h] = p.astype(BF16)
            alphas.append(alpha)
        vt = v_ref[0, pl.ds(k0, t), :].T
        for h in range(HC):
            acc_sc[h][...] = alphas[h] * acc_sc[h][...] + jnp.dot(
                vt[h * DC:(h + 1) * DC, :], p_sc[h], preferred_element_type=F32)
        return carry

    lax.fori_loop(0, nblk, att_step, 0)
    heads = [acc_sc[h][...] / l_sc[h][...] for h in range(HC)]
    o_ref[0] = jnp.concatenate(heads, axis=0).T.astype(BF16)


def _dsa_attn(hm3, hik3, hw3, slopes):
    b, s, _ = hm3.shape
    t = DSA_T
    w = HC * DC
    cq, ck, cv = 9, 10, 11
    body = functools.partial(_dsa_body, n_sel=min(TOPK_MAX, s // 4), slopes=slopes)
    return pl.pallas_call(
        body,
        grid=(b, s // t),
        in_specs=[pl.BlockSpec((1, t, w), lambda bi, i: (bi, i, cq)),
                  pl.BlockSpec((1, s, w), lambda bi, i: (bi, 0, ck)),
                  pl.BlockSpec((1, s, w), lambda bi, i: (bi, 0, cv)),
                  pl.BlockSpec((1, t, HI * DI), lambda bi, i: (bi, i, 0)),
                  pl.BlockSpec((1, s, HI * DI), lambda bi, i: (bi, 0, 1)),
                  pl.BlockSpec((1, t, D_W), lambda bi, i: (bi, i, 0))],
        out_specs=pl.BlockSpec((1, t, w), lambda bi, i: (bi, i, 0)),
        out_shape=jax.ShapeDtypeStruct((b, s, w), BF16),
        scratch_shapes=([pltpu.VMEM((s, t), I32), pltpu.VMEM((s, t), F32),
                         pltpu.VMEM((HC, t, t), F32), pltpu.VMEM((HC, t, t), BF16)]
                        + [pltpu.VMEM((1, t), F32)] * (2 * HC) + [pltpu.VMEM((DC, t), F32)] * HC),
        compiler_params=_params(("parallel", "arbitrary")),
        name="dsa_attn",
    )(hm3, hm3, hm3, hik3, hik3, hw3)


def _layer_norm(y, g, b):
    mu = jnp.mean(y, axis=-1, keepdims=True)
    d = y - mu
    var = jnp.mean(d * d, axis=-1, keepdims=True)
    return d * lax.rsqrt(var + EPS) * g + b


def _out_router_body(a_ref, b_ref, c_ref, x_ref, wo_ref, g_ref, beta_ref, wr_ref, br_ref,
                     x1_ref, x1b_ref, meta_ref, cnt_ref, *, tm):
    i = pl.program_id(0)
    wa = HA * 2 * DA
    wb = wa + HB * DB
    y = (jnp.dot(a_ref[...], wo_ref[0:wa, :], preferred_element_type=F32)
         + jnp.dot(b_ref[...], wo_ref[wa:wb, :], preferred_element_type=F32)
         + jnp.dot(c_ref[...], wo_ref[wb:, :], preferred_element_type=F32))
    x1 = _layer_norm(DEEPNORM_ALPHA * x_ref[...] + y, g_ref[...], beta_ref[...])
    x1_ref[...] = x1
    x1b_ref[...] = x1.astype(BF16)

    logits = jnp.dot(x1, wr_ref[...], preferred_element_type=F32,
                     precision=lax.Precision.HIGHEST) + br_ref[...]
    lane_e = lax.broadcasted_iota(I32, (tm, N_EXPERTS), 1).astype(F32)
    sel = jnp.zeros((tm, N_EXPERTS), F32)
    work = logits
    picks, vals = [], []
    for _ in range(TOP_K):
        v = jnp.max(work, axis=-1, keepdims=True)
        e = jnp.min(jnp.where(work == v, lane_e, float(N_EXPERTS)), axis=-1, keepdims=True)
        hit = lane_e == e
        sel = jnp.where(hit, 1.0, sel)
        work = jnp.where(hit, -jnp.inf, work)
        picks.append(e)
        vals.append(v)
    ex = [jnp.exp(v - vals[0]) for v in vals]
    den = ex[0] + ex[1] + ex[2] + ex[3]

    @pl.when(i == 0)
    def _():
        cnt_ref[...] = jnp.zeros_like(cnt_ref)

    tri = (lax.broadcasted_iota(I32, (tm, tm), 1) < lax.broadcasted_iota(I32, (tm, tm), 0)).astype(BF16)
    before = jnp.dot(tri, sel.astype(BF16), preferred_element_type=F32) + cnt_ref[0:1, 0:N_EXPERTS]
    cnt_ref[0:1, 0:N_EXPERTS] = cnt_ref[0:1, 0:N_EXPERTS] + jnp.sum(sel, axis=0, keepdims=True)

    lane = lax.broadcasted_iota(I32, (tm, 128), 1)
    meta = jnp.zeros((tm, 128), F32)
    for kk in range(TOP_K):
        rank = jnp.sum(jnp.where(lane_e == picks[kk], before, 0.0), axis=-1, keepdims=True)
        meta = jnp.where(lane == kk, picks[kk], meta)
        meta = jnp.where(lane == TOP_K + kk, rank, meta)
        meta = jnp.where(lane == 2 * TOP_K + kk, ex[kk] / den, meta)
    meta_ref[...] = meta


def _out_router(oa, ob, oc, x2d, wo, g, beta, wr, br):
    t = x2d.shape[0]
    tm = 512
    row = lambda i: (i, 0)
    fix = lambda i: (0, 0)
    body = functools.partial(_out_router_body, tm=tm)
    return pl.pallas_call(
        body,
        grid=(t // tm,),
        in_specs=[pl.BlockSpec((tm, oa.shape[1]), row), pl.BlockSpec((tm, ob.shape[1]), row),
                  pl.BlockSpec((tm, oc.shape[1]), row), pl.BlockSpec((tm, D_MODEL), row),
                  pl.BlockSpec((D_MIX, D_MODEL), fix), pl.BlockSpec((1, D_MODEL), fix),
                  pl.BlockSpec((1, D_MODEL), fix), pl.BlockSpec((D_MODEL, N_EXPERTS), fix),
                  pl.BlockSpec((1, N_EXPERTS), fix)],
        out_specs=[pl.BlockSpec((tm, D_MODEL), row), pl.BlockSpec((tm, D_MODEL), row),
                   pl.BlockSpec((tm, 128), row), pl.BlockSpec((8, 128), fix)],
        out_shape=[jax.ShapeDtypeStruct((t, D_MODEL), F32), jax.ShapeDtypeStruct((t, D_MODEL), BF16),
                   jax.ShapeDtypeStruct((t, 128), F32), jax.ShapeDtypeStruct((8, 128), F32)],
        compiler_params=_params(("arbitrary",)),
        name="out_router",
    )(oa, ob, oc, x2d, wo, g, beta, wr, br)


MOE_ROWS = 256


def _moe_body(be_ref, bv_ref, x_ref, wgu_ref, bgu_ref, wd_ref, bd_ref, y_ref, wgu_sc, wd_sc):
    i = pl.program_id(0)
    prev = be_ref[jnp.maximum(i - 1, 0)]

    @pl.when((i == 0) | (be_ref[i] != prev))
    def _():
        wgu_sc[...] = wgu_ref[0, 0].astype(BF16)
        wd_sc[...] = wd_ref[0, 0].astype(BF16)

    @pl.when(bv_ref[i] > 0)
    def _():
        h = jnp.dot(x_ref[...], wgu_sc[...], preferred_element_type=F32) + bgu_ref[0, 0]
        gate = jnp.minimum(h[:, :D_EXPERT], SWIGLU_LIMIT)
        up = jnp.clip(h[:, D_EXPERT:], -SWIGLU_LIMIT, SWIGLU_LIMIT)
        glu = gate * (1.0 / (1.0 + jnp.exp(-SWIGLU_ALPHA * gate)))
        act = ((up + 1.0) * glu).astype(BF16)
        y_ref[...] = jnp.dot(act, wd_sc[...], preferred_element_type=F32) + bd_ref[0, 0]

    @pl.when(bv_ref[i] == 0)
    def _():
        y_ref[...] = jnp.zeros_like(y_ref)


def _moe(layer, block_expert, block_valid, x_rows, wgu, bgu, wd, bd):
    n_rows = x_rows.shape[0]
    expert = lambda i, be, bv: (layer, be[i], 0, 0)
    grid_spec = pltpu.PrefetchScalarGridSpec(
        num_scalar_prefetch=2,
        grid=(n_rows // MOE_ROWS,),
        in_specs=[pl.BlockSpec((MOE_ROWS, D_MODEL), lambda i, be, bv: (i, 0)),
                  pl.BlockSpec((1, 1, D_MODEL, 2 * D_EXPERT), expert),
                  pl.BlockSpec((1, 1, 1, 2 * D_EXPERT), expert),
                  pl.BlockSpec((1, 1, D_EXPERT, D_MODEL), expert),
                  pl.BlockSpec((1, 1, 1, D_MODEL), expert)],
        out_specs=pl.BlockSpec((MOE_ROWS, D_MODEL), lambda i, be, bv: (i, 0)),
        scratch_shapes=[pltpu.VMEM((D_MODEL, 2 * D_EXPERT), BF16), pltpu.VMEM((D_EXPERT, D_MODEL), BF16)],
    )
    return pl.pallas_call(
        _moe_body,
        grid_spec=grid_spec,
        out_shape=jax.ShapeDtypeStruct((n_rows, D_MODEL), F32),
        compiler_params=_params(("arbitrary",)),
        name="moe_experts",
    )(block_expert, block_valid, x_rows, wgu, bgu, wd, bd)


def _combine_body(x_ref, y0_ref, y1_ref, y2_ref, y3_ref, meta_ref, g_ref, beta_ref, o_ref):
    gates = meta_ref[...]
    ffn = (gates[:, 8:9] * y0_ref[...] + gates[:, 9:10] * y1_ref[...]
           + gates[:, 10:11] * y2_ref[...] + gates[:, 11:12] * y3_ref[...])
    o_ref[...] = _layer_norm(DEEPNORM_ALPHA * x_ref[...] + ffn, g_ref[...], beta_ref[...])


def _combine(x1, ys, meta, g, beta):
    t = x1.shape[0]
    tm = 512
    row = lambda i: (i, 0)
    fix = lambda i: (0, 0)
    return pl.pallas_call(
        _combine_body,
        grid=(t // tm,),
        in_specs=[pl.BlockSpec((tm, D_MODEL), row)] * 5
                 + [pl.BlockSpec((tm, 128), row), pl.BlockSpec((1, D_MODEL), fix),
                    pl.BlockSpec((1, D_MODEL), fix)],
        out_specs=pl.BlockSpec((tm, D_MODEL), row),
        out_shape=jax.ShapeDtypeStruct((t, D_MODEL), F32),
        compiler_params=_params(("parallel",)),
        name="combine_ln",
    )(x1, *ys, meta, g, beta)


def _prep_w_in(w):
    scale = DA ** -0.5
    col = jnp.ones((D_QKV,), F32)
    for lo, width in ((0, HA * 2 * DA), (3 * HA * 2 * DA, HB * DB),
                      (3 * HA * 2 * DA + 3 * HB * DB, HC * DC)):
        col = col.at[lo:lo + width].set(scale)
    main = w[:, :D_QKV] * col
    q_idx = w[:, D_QKV:D_QKV + HI * DI]
    k_idx = w[:, D_QKV + HI * DI:D_QKV + HI * DI + DI]
    w_idx = w[:, D_QKV + HI * DI + DI:] * ((HI ** -0.5) * (DI ** -0.5))
    w_pad = jnp.zeros((D_MODEL, D_W - HI), F32)
    return jnp.concatenate([main, q_idx, jnp.tile(k_idx, (1, HI)), w_idx, w_pad], axis=1).astype(BF16)


def _route(meta, cnt, n_tok):
    top_e = meta[:, 0:TOP_K].astype(I32)
    rank = meta[:, TOP_K:2 * TOP_K].astype(I32)
    counts = cnt[0, :N_EXPERTS].astype(I32)
    n_blocks = n_tok * TOP_K // MOE_ROWS + N_EXPERTS
    blocks_e = (counts + MOE_ROWS - 1) // MOE_ROWS
    blocks_end = jnp.cumsum(blocks_e)
    row_start = (blocks_end - blocks_e) * MOE_ROWS
    dest = row_start[top_e] + rank
    tok = jnp.broadcast_to(jnp.arange(n_tok, dtype=I32)[:, None], (n_tok, TOP_K))
    row_token = jnp.zeros((n_blocks * MOE_ROWS,), I32).at[dest.reshape(-1)].set(
        tok.reshape(-1), unique_indices=True)
    blk = jnp.arange(n_blocks, dtype=I32)
    block_expert = jnp.minimum(jnp.sum((blocks_end[None, :] <= blk[:, None]).astype(I32), axis=1),
                               N_EXPERTS - 1)
    block_valid = (blk < blocks_end[-1]).astype(I32)
    return dest, row_token, block_expert, block_valid


def kernel(x, w_in, lam_q1, lam_k1, lam_q2, lam_k2, subln_g, rel_bias, w_out, ln1_g, ln1_b,
           w_router, b_router, w_gu, b_gu, w_down, b_down, ln2_g, ln2_b):
    bsz, seq, _ = x.shape
    n_tok = bsz * seq
    slopes = [2.0 ** (-8.0 * i / (HA + HC)) for i in range(1, HA + HC + 1)]
    slopes_a = tuple(slopes[0::2])
    slopes_c = tuple(slopes[1::2])
    x2d = x.reshape(n_tok, D_MODEL)
    for l in range(DEPTH):
        lam_init = 0.8 - 0.6 * math.exp(-0.3 * l)
        lam = (jnp.exp(jnp.sum(lam_q1[l] * lam_k1[l])) - jnp.exp(jnp.sum(lam_q2[l] * lam_k2[l]))
               + lam_init).reshape(1).astype(F32)
        hm, hik, hw = _in_proj(x2d, _prep_w_in(w_in[l]))
        hm3 = hm.reshape(bsz, seq, D_QKV)
        out_a = _diff_attn(hm3, slopes_a, lam, subln_g[l].reshape(1, 2 * DA), lam_init)
        out_b = _band_attn(hm3, _band_bias(rel_bias[l]))
        out_c = _dsa_attn(hm3, hik.reshape(bsz, seq, D_IK), hw.reshape(bsz, seq, D_W), slopes_c)
        x1, x1b, meta, cnt = _out_router(
            out_a.reshape(n_tok, -1), out_b.reshape(n_tok, -1), out_c.reshape(n_tok, -1), x2d,
            w_out[l].astype(BF16), ln1_g[l].reshape(1, -1), ln1_b[l].reshape(1, -1),
            w_router[l], b_router[l].reshape(1, -1))
        dest, row_token, block_expert, block_valid = _route(meta, cnt, n_tok)
        x_rows = x1b[row_token]
        y_rows = _moe(l, block_expert, block_valid, x_rows, w_gu, b_gu.reshape(DEPTH, N_EXPERTS, 1, -1),
                      w_down, b_down.reshape(DEPTH, N_EXPERTS, 1, -1))
        ys = [y_rows[dest[:, kk]] for kk in range(TOP_K)]
        x2d = _combine(x1, ys, meta, ln2_g[l].reshape(1, -1), ln2_b[l].reshape(1, -1))
    return x2d.reshape(bsz, seq, D_MODEL)
```

```python
import functools
import math

import jax
import jax.numpy as jnp
from jax import lax
from jax.experimental import pallas as pl
from jax.experimental.pallas import tpu as pltpu

F32 = jnp.float32
BF16 = jnp.bfloat16
I32 = jnp.int32

D_MODEL = 1024
DEPTH = 4
CHUNK = 64
CHUNK_SHIFT = 6
HA, DA = 4, 64
HB, DB = 4, 64
BAND_CHUNKS = 8
REL_CLIP = 128
HC, DC = 4, 64
HI, DI = 8, 32
TOPK_MAX = 256
N_EXPERTS = 32
TOP_K = 4
D_EXPERT = D_MODEL
SWIGLU_LIMIT = 7.0
SWIGLU_ALPHA = 1.702
DEEPNORM_ALPHA = (2 * DEPTH) ** 0.25
EPS = 1e-5
NEG_INF = -1e30

D_QKV = 3 * HA * 2 * DA + 3 * HB * DB + 3 * HC * DC
D_IK = 2 * HI * DI
D_W = 128
D_MIX = HA * 2 * DA + HB * DB + HC * DC

VMEM_LIMIT = 56 * 1024 * 1024

_NT = (((1,), (1,)), ((), ()))
_TN = (((0,), (0,)), ((), ()))


def _params(sem):
    return pltpu.CompilerParams(dimension_semantics=sem, vmem_limit_bytes=VMEM_LIMIT)


def _col_sum32(x):
    n, t = x.shape
    return x.reshape(n // 32, 32, t).sum(axis=0)


def _in_proj_body(x_ref, w_ref, hm_ref, hik_ref, hw_ref):
    xb = x_ref[...].astype(BF16)
    for n0 in range(0, D_QKV, 512):
        hm_ref[:, n0:n0 + 512] = jnp.dot(
            xb, w_ref[:, n0:n0 + 512], preferred_element_type=F32).astype(BF16)
    hik_ref[...] = jnp.dot(xb, w_ref[:, D_QKV:D_QKV + D_IK], preferred_element_type=F32).astype(BF16)
    hw_ref[...] = jnp.dot(xb, w_ref[:, D_QKV + D_IK:], preferred_element_type=F32)


def _in_proj(x2d, w):
    t = x2d.shape[0]
    tm = 512
    dw = D_QKV + D_IK + D_W
    return pl.pallas_call(
        _in_proj_body,
        grid=(t // tm,),
        in_specs=[pl.BlockSpec((tm, D_MODEL), lambda i: (i, 0)),
                  pl.BlockSpec((D_MODEL, dw), lambda i: (0, 0))],
        out_specs=[pl.BlockSpec((tm, D_QKV), lambda i: (i, 0)),
                   pl.BlockSpec((tm, D_IK), lambda i: (i, 0)),
                   pl.BlockSpec((tm, D_W), lambda i: (i, 0))],
        out_shape=[jax.ShapeDtypeStruct((t, D_QKV), BF16),
                   jax.ShapeDtypeStruct((t, D_IK), BF16),
                   jax.ShapeDtypeStruct((t, D_W), F32)],
        compiler_params=_params(("parallel",)),
        name="in_proj",
    )(x2d, w)


def _diff_attn_body(lam_ref, q_ref, k_ref, v_ref, g_ref, o_ref,
                    bias_sc, z_sc, p_sc, *state, t, slopes, out_scale):
    qi = pl.program_id(1)
    lam = lam_ref[0]
    dh = 2 * DA
    lane = lax.broadcasted_iota(I32, (1, dh), 1)
    q_half = []
    for h in range(HA):
        qh = q_ref[0, :, h * dh:(h + 1) * dh]
        zero = jnp.zeros_like(qh)
        q_half.append((jnp.where(lane < DA, qh, zero), jnp.where(lane >= DA, qh, zero)))

    kk = lax.broadcasted_iota(I32, (t, t), 0)
    tt = lax.broadcasted_iota(I32, (t, t), 1)
    own_chunk = (kk >> CHUNK_SHIFT) <= (tt >> CHUNK_SHIFT)
    for h in range(HA):
        bias_sc[h] = slopes[h] * kk.astype(F32)
        bias_sc[HA + h] = jnp.where(own_chunk, slopes[h] * jnp.minimum(kk, 2 * tt - kk).astype(F32), NEG_INF)

    n_chain = 2 * HA
    m_sc, l_sc, acc_sc = state[:n_chain], state[n_chain:2 * n_chain], state[2 * n_chain:]
    for c in range(n_chain):
        m_sc[c][...] = jnp.full((1, t), NEG_INF, F32)
        l_sc[c][...] = jnp.zeros((1, t), F32)
        acc_sc[c][...] = jnp.zeros((dh, t), F32)

    def kv_block(j, bias_base):
        k0 = pl.multiple_of(j * t, t)
        k0f = k0.astype(F32)
        zmax = []
        for h in range(HA):
            kb = k_ref[0, pl.ds(k0, t), h * dh:(h + 1) * dh]
            bias = bias_sc[bias_base + h]
            for a in range(2):
                z = lax.dot_general(kb, q_half[h][a], _NT, preferred_element_type=F32) + bias
                z_sc[2 * h + a] = z
                zmax.append(jnp.max(z, axis=0, keepdims=True))
        alphas = []
        for c in range(n_chain):
            shift = slopes[c // 2] * k0f
            m_old = m_sc[c][...]
            m_new = jnp.maximum(m_old, zmax[c] + shift)
            alpha = jnp.exp(m_old - m_new)
            p = jnp.exp(z_sc[c] - (m_new - shift))
            l_sc[c][...] = alpha * l_sc[c][...] + jnp.sum(p, axis=0, keepdims=True)
            m_sc[c][...] = m_new
            p_sc[c] = p.astype(BF16)
            alphas.append(alpha)
        for h in range(HA):
            vt = v_ref[0, pl.ds(k0, t), h * dh:(h + 1) * dh].T
            for a in range(2):
                c = 2 * h + a
                acc_sc[c][...] = alphas[c] * acc_sc[c][...] + jnp.dot(
                    vt, p_sc[c], preferred_element_type=F32)

    def past_step(j, carry):
        kv_block(j, 0)
        return carry

    lax.fori_loop(0, qi, past_step, 0)
    kv_block(qi, HA)

    outs = []
    for h in range(HA):
        ot = (acc_sc[2 * h][...] / l_sc[2 * h][...]
              - lam * (acc_sc[2 * h + 1][...] / l_sc[2 * h + 1][...]))
        ms = jnp.mean(ot * ot, axis=0, keepdims=True)
        outs.append((ot * lax.rsqrt(ms + EPS)).T * g_ref[...] * out_scale)
    o_ref[0] = jnp.concatenate(outs, axis=1).astype(BF16)


def _diff_attn(hm3, slopes, lam, sub_g, lam_init):
    b, s, _ = hm3.shape
    t = 256
    dh = 2 * DA
    w = HA * dh
    body = functools.partial(_diff_attn_body, t=t, slopes=slopes, out_scale=1.0 - lam_init)
    return pl.pallas_call(
        body,
        grid=(b, s // t),
        in_specs=[pl.BlockSpec(memory_space=pltpu.SMEM),
                  pl.BlockSpec((1, t, w), lambda bi, qi: (bi, qi, 0)),
                  pl.BlockSpec((1, s, w), lambda bi, qi: (bi, 0, 1)),
                  pl.BlockSpec((1, s, w), lambda bi, qi: (bi, 0, 2)),
                  pl.BlockSpec((1, dh), lambda bi, qi: (0, 0))],
        out_specs=pl.BlockSpec((1, t, w), lambda bi, qi: (bi, qi, 0)),
        out_shape=jax.ShapeDtypeStruct((b, s, w), BF16),
        scratch_shapes=([pltpu.VMEM((2 * HA, t, t), F32), pltpu.VMEM((2 * HA, t, t), F32),
                         pltpu.VMEM((2 * HA, t, t), BF16)]
                        + [pltpu.VMEM((1, t), F32)] * (4 * HA) + [pltpu.VMEM((dh, t), F32)] * (2 * HA)),
        compiler_params=_params(("parallel", "arbitrary")),
        name="diff_attn",
    )(lam, hm3, hm3, hm3, sub_g)


BAND_TQ = 256
BAND_TK = 3 * BAND_TQ


def _band_attn_body(q_ref, k0_ref, k1_ref, k2_ref, v0_ref, v1_ref, v2_ref, bias_ref, o_ref):
    i = pl.program_id(1)
    q = q_ref[0]
    k = jnp.concatenate([k0_ref[0], k1_ref[0], k2_ref[0]], axis=0)
    v = jnp.concatenate([v0_ref[0], v1_ref[0], v2_ref[0]], axis=0)
    col = lax.broadcasted_iota(I32, (1, BAND_TK), 1)
    first_valid = jnp.where(i >= 2, 0, jnp.where(i >= 1, BAND_TQ, 2 * BAND_TQ))
    kvalid = col >= first_valid
    lane = lax.broadcasted_iota(I32, (1, HB * DB), 1)
    zero = jnp.zeros_like(q)
    out = jnp.zeros((BAND_TQ, HB * DB), F32)
    for h in range(HB):
        head = (lane >> 6) == h
        s = lax.dot_general(jnp.where(head, q, zero), k, _NT, preferred_element_type=F32)
        s = jnp.where(kvalid, s + bias_ref[h], NEG_INF)
        m = jnp.max(s, axis=-1, keepdims=True)
        p = jnp.exp(s - m)
        l = jnp.sum(p, axis=-1, keepdims=True)
        o = jnp.dot(p.astype(BF16), v, preferred_element_type=F32) / l
        out = jnp.where(head, o, out)
    o_ref[0] = out.astype(BF16)


def _band_bias(rel_bias):
    r = jnp.arange(BAND_TQ)[:, None]
    j = jnp.arange(BAND_TK)[None, :]
    rel = 2 * BAND_TQ + r - j
    qc = r // CHUNK
    kc = j // CHUNK
    in_band = (kc >= qc) & (kc <= qc + BAND_CHUNKS)
    bias = rel_bias.astype(F32)[:, jnp.clip(rel, -REL_CLIP, REL_CLIP) + REL_CLIP]
    return jnp.where(in_band[None], bias, NEG_INF)


def _band_attn(hm3, bias):
    b, s, _ = hm3.shape
    w = HB * DB
    cq, ck, cv = 6, 7, 8

    def kv_spec(col, back):
        return pl.BlockSpec((1, BAND_TQ, w), lambda bi, i: (bi, jnp.maximum(i - back, 0), col))

    return pl.pallas_call(
        _band_attn_body,
        grid=(b, s // BAND_TQ),
        in_specs=[pl.BlockSpec((1, BAND_TQ, w), lambda bi, i: (bi, i, cq)),
                  kv_spec(ck, 2), kv_spec(ck, 1), kv_spec(ck, 0),
                  kv_spec(cv, 2), kv_spec(cv, 1), kv_spec(cv, 0),
                  pl.BlockSpec((HB, BAND_TQ, BAND_TK), lambda bi, i: (0, 0, 0))],
        out_specs=pl.BlockSpec((1, BAND_TQ, w), lambda bi, i: (bi, i, 0)),
        out_shape=jax.ShapeDtypeStruct((b, s, w), BF16),
        compiler_params=_params(("parallel", "parallel")),
        name="band_attn",
    )(hm3, hm3, hm3, hm3, hm3, hm3, hm3, bias)


INT_MIN = -2 ** 31
NEG_INF_KEY = -1900671691
UNSELECTED = -1e33
DSA_T = 256


def _order_key(x):
    bits = lax.bitcast_convert_type(x, I32)
    return jnp.where(bits < 0, bits ^ jnp.int32(0x7FFFFFFF), bits)


def _dsa_body(q_ref, k_ref, v_ref, qi_ref, ki_ref, w_ref, o_ref, key_sc, base_sc, z_sc, p_sc, *state,
              n_sel, slopes):
    t = DSA_T
    qi = pl.program_id(1)
    q0 = qi * t
    nblk = qi + 1
    kk = lax.broadcasted_iota(I32, (t, t), 0)
    tg = q0 + lax.broadcasted_iota(I32, (t, t), 1)

    def block_start(j):
        return pl.multiple_of(j * t, t)

    qidx = qi_ref[0]
    lane_i = lax.broadcasted_iota(I32, (1, HI * DI), 1)
    zero_i = jnp.zeros_like(qidx)
    qm = [jnp.where((lane_i >> 5) == h, qidx, zero_i) for h in range(HI)]
    wt = w_ref[0].T

    def score_step(j, carry):
        k0 = block_start(j)
        kib = ki_ref[0, pl.ds(k0, t), :]
        sc = jnp.zeros((t, t), F32)
        for h in range(HI):
            rel = jnp.maximum(lax.dot_general(kib, qm[h], _NT, preferred_element_type=F32), 0.0)
            sc = sc + wt[h:h + 1, :] * rel
        allowed = ((k0 + kk) >> CHUNK_SHIFT) <= (tg >> CHUNK_SHIFT)
        key_sc[pl.ds(k0, t), :] = _order_key(jnp.where(allowed, sc, NEG_INF))
        return carry

    lax.fori_loop(0, nblk, score_step, 0)

    kf = float(n_sel)

    def count_ge(cand):
        def blk(j, acc):
            kb = key_sc[pl.ds(block_start(j), t), :]
            return acc + _col_sum32(jnp.where(kb >= cand, 1.0, 0.0))
        acc = lax.fori_loop(0, nblk, blk, jnp.zeros((32, t), F32))
        return jnp.sum(acc, axis=0, keepdims=True)

    thr = jnp.where(count_ge(jnp.zeros((1, t), I32)) >= kf, 0, INT_MIN).astype(I32)

    def bit_step(i, thr):
        cand = thr + jnp.left_shift(jnp.int32(1), 30 - i)
        return jnp.where(count_ge(cand) >= kf, cand, thr)

    thr = lax.fori_loop(0, 31, bit_step, thr)

    def alibi_or_unselected(keep, k0):
        kg = k0 + kk
        allowed = (kg >> CHUNK_SHIFT) <= (tg >> CHUNK_SHIFT)
        return jnp.where(keep, jnp.where(allowed, -jnp.abs(tg - kg).astype(F32), UNSELECTED), UNSELECTED)

    def select_step(j, n_ge):
        k0 = block_start(j)
        ge = key_sc[pl.ds(k0, t), :] >= thr
        base_sc[pl.ds(k0, t), :] = alibi_or_unselected(ge, k0)
        return n_ge + _col_sum32(jnp.where(ge, 1.0, 0.0))

    n_ge = jnp.sum(lax.fori_loop(0, nblk, select_step, jnp.zeros((32, t), F32)), axis=0, keepdims=True)
    over = jnp.where(n_ge > kf, jnp.where(thr != NEG_INF_KEY, 1.0, 0.0), 0.0)

    @pl.when(jnp.max(over) > 0.0)
    def _():
        def gt_step(j, acc):
            kb = key_sc[pl.ds(block_start(j), t), :]
            return acc + _col_sum32(jnp.where(kb > thr, 1.0, 0.0))
        n_gt = jnp.sum(lax.fori_loop(0, nblk, gt_step, jnp.zeros((32, t), F32)), axis=0, keepdims=True)
        need = kf - n_gt
        earlier = jnp.where(lax.broadcasted_iota(I32, (t, t), 1) < kk, 1.0, 0.0).astype(BF16)

        def tie_step(j, offs):
            k0 = block_start(j)
            kb = key_sc[pl.ds(k0, t), :]
            e = jnp.where(kb == thr, 1.0, 0.0)
            rank = jnp.dot(earlier, e.astype(BF16), preferred_element_type=F32) + offs
            keep = jnp.where(kb > thr, 1.0, jnp.where(rank < need, e, 0.0))
            base_sc[pl.ds(k0, t), :] = alibi_or_unselected(keep > 0.0, k0)
            return offs + jnp.sum(e, axis=0, keepdims=True)

        lax.fori_loop(0, nblk, tie_step, jnp.zeros((1, t), F32))

    q = q_ref[0]
    lane = lax.broadcasted_iota(I32, (1, HC * DC), 1)
    zero = jnp.zeros_like(q)
    q_head = [jnp.where((lane >> 6) == h, q, zero) for h in range(HC)]
    m_sc, l_sc, acc_sc = state[:HC], state[HC:2 * HC], state[2 * HC:]
    for h in range(HC):
        m_sc[h][...] = jnp.full((1, t), NEG_INF, F32)
        l_sc[h][...] = jnp.zeros((1, t), F32)
        acc_sc[h][...] = jnp.zeros((DC, t), F32)

    def att_step(j, carry):
        k0 = block_start(j)
        kb = k_ref[0, pl.ds(k0, t), :]
        base = base_sc[pl.ds(k0, t), :]
        zmax = []
        for h in range(HC):
            z = lax.dot_general(kb, q_head[h], _NT, preferred_element_type=F32) + slopes[h] * base
            z_sc[h] = z
            zmax.append(jnp.max(z, axis=0, keepdims=True))
        alphas = []
        for h in range(HC):
            m_old = m_sc[h][...]
            m_new = jnp.maximum(m_old, zmax[h])
            alpha = jnp.exp(m_old - m_new)
            p = jnp.exp(z_sc[h] - m_new)
            l_sc[h][...] = alpha * l_sc[h][...] + jnp.sum(p, axis=0, keepdims=True)
            m_sc[h][...] = m_new
            p_sc[h] = p.astype(BF16)
            alphas.append(alpha)
        vt = v_ref[0, pl.ds(k0, t), :].T
        for h in range(HC):
            acc_sc[h][...] = alphas[h] * acc_sc[h][...] + jnp.dot(
                vt[h * DC:(h + 1) * DC, :], p_sc[h], preferred_element_type=F32)
        return carry

    lax.fori_loop(0, nblk, att_step, 0)
    heads = [acc_sc[h][...] / l_sc[h][...] for h in range(HC)]
    o_ref[0] = jnp.concatenate(heads, axis=0).T.astype(BF16)


def _dsa_attn(hm3, hik3, hw3, slopes):
    b, s, _ = hm3.shape
    t = DSA_T
    w = HC * DC
    cq, ck, cv = 9, 10, 11
    body = functools.partial(_dsa_body, n_sel=min(TOPK_MAX, s // 4), slopes=slopes)
    return pl.pallas_call(
        body,
        grid=(b, s // t),
        in_specs=[pl.BlockSpec((1, t, w), lambda bi, i: (bi, i, cq)),
                  pl.BlockSpec((1, s, w), lambda bi, i: (bi, 0, ck)),
                  pl.BlockSpec((1, s, w), lambda bi, i: (bi, 0, cv)),
                  pl.BlockSpec((1, t, HI * DI), lambda bi, i: (bi, i, 0)),
                  pl.BlockSpec((1, s, HI * DI), lambda bi, i: (bi, 0, 1)),
                  pl.BlockSpec((1, t, D_W), lambda bi, i: (bi, i, 0))],
        out_specs=pl.BlockSpec((1, t, w), lambda bi, i: (bi, i, 0)),
        out_shape=jax.ShapeDtypeStruct((b, s, w), BF16),
        scratch_shapes=([pltpu.VMEM((s, t), I32), pltpu.VMEM((s, t), F32),
                         pltpu.VMEM((HC, t, t), F32), pltpu.VMEM((HC, t, t), BF16)]
                        + [pltpu.VMEM((1, t), F32)] * (2 * HC) + [pltpu.VMEM((DC, t), F32)] * HC),
        compiler_params=_params(("parallel", "arbitrary")),
        name="dsa_attn",
    )(hm3, hm3, hm3, hik3, hik3, hw3)


def _layer_norm(y, g, b):
    mu = jnp.mean(y, axis=-1, keepdims=True)
    d = y - mu
    var = jnp.mean(d * d, axis=-1, keepdims=True)
    return d * lax.rsqrt(var + EPS) * g + b


def _out_router_body(a_ref, b_ref, c_ref, x_ref, wo_ref, g_ref, beta_ref, wr_ref, br_ref,
                     x1_ref, meta_ref, cnt_ref, *, tm):
    i = pl.program_id(0)
    wa = HA * 2 * DA
    wb = wa + HB * DB
    y = (jnp.dot(a_ref[...], wo_ref[0:wa, :], preferred_element_type=F32)
         + jnp.dot(b_ref[...], wo_ref[wa:wb, :], preferred_element_type=F32)
         + jnp.dot(c_ref[...], wo_ref[wb:, :], preferred_element_type=F32))
    x1 = _layer_norm(DEEPNORM_ALPHA * x_ref[...] + y, g_ref[...], beta_ref[...])
    x1_ref[...] = x1

    logits = jnp.dot(x1, wr_ref[...], preferred_element_type=F32,
                     precision=lax.Precision.HIGHEST) + br_ref[...]
    lane_e = lax.broadcasted_iota(I32, (tm, N_EXPERTS), 1).astype(F32)
    sel = jnp.zeros((tm, N_EXPERTS), F32)
    work = logits
    picks, vals = [], []
    for _ in range(TOP_K):
        v = jnp.max(work, axis=-1, keepdims=True)
        e = jnp.min(jnp.where(work == v, lane_e, float(N_EXPERTS)), axis=-1, keepdims=True)
        hit = lane_e == e
        sel = jnp.where(hit, 1.0, sel)
        work = jnp.where(hit, -jnp.inf, work)
        picks.append(e)
        vals.append(v)
    ex = [jnp.exp(v - vals[0]) for v in vals]
    den = ex[0] + ex[1] + ex[2] + ex[3]

    @pl.when(i == 0)
    def _():
        cnt_ref[...] = jnp.zeros_like(cnt_ref)

    tri = (lax.broadcasted_iota(I32, (tm, tm), 1) < lax.broadcasted_iota(I32, (tm, tm), 0)).astype(BF16)
    before = jnp.dot(tri, sel.astype(BF16), preferred_element_type=F32) + cnt_ref[0:1, 0:N_EXPERTS]
    cnt_ref[0:1, 0:N_EXPERTS] = cnt_ref[0:1, 0:N_EXPERTS] + jnp.sum(sel, axis=0, keepdims=True)

    lane = lax.broadcasted_iota(I32, (tm, 128), 1)
    meta = jnp.zeros((tm, 128), F32)
    for kk in range(TOP_K):
        rank = jnp.sum(jnp.where(lane_e == picks[kk], before, 0.0), axis=-1, keepdims=True)
        meta = jnp.where(lane == kk, picks[kk], meta)
        meta = jnp.where(lane == TOP_K + kk, rank, meta)
        meta = jnp.where(lane == 2 * TOP_K + kk, ex[kk] / den, meta)
    meta_ref[...] = meta


def _out_router(oa, ob, oc, x2d, wo, g, beta, wr, br):
    t = x2d.shape[0]
    tm = 512
    row = lambda i: (i, 0)
    fix = lambda i: (0, 0)
    body = functools.partial(_out_router_body, tm=tm)
    return pl.pallas_call(
        body,
        grid=(t // tm,),
        in_specs=[pl.BlockSpec((tm, oa.shape[1]), row), pl.BlockSpec((tm, ob.shape[1]), row),
                  pl.BlockSpec((tm, oc.shape[1]), row), pl.BlockSpec((tm, D_MODEL), row),
                  pl.BlockSpec((D_MIX, D_MODEL), fix), pl.BlockSpec((1, D_MODEL), fix),
                  pl.BlockSpec((1, D_MODEL), fix), pl.BlockSpec((D_MODEL, N_EXPERTS), fix),
                  pl.BlockSpec((1, N_EXPERTS), fix)],
        out_specs=[pl.BlockSpec((tm, D_MODEL), row),
                   pl.BlockSpec((tm, 128), row), pl.BlockSpec((8, 128), fix)],
        out_shape=[jax.ShapeDtypeStruct((t, D_MODEL), F32),
                   jax.ShapeDtypeStruct((t, 128), F32), jax.ShapeDtypeStruct((8, 128), F32)],
        compiler_params=_params(("arbitrary",)),
        name="out_router",
    )(oa, ob, oc, x2d, wo, g, beta, wr, br)


MOE_ROWS = 256
DISPATCH_TOKENS = 256


def _dispatch_body(dest_ref, x_ref, rows_in, rows_out, sem):
    del rows_in

    def token(r, carry):
        for kk in range(TOP_K):
            d = dest_ref[r * TOP_K + kk]
            pltpu.make_async_copy(x_ref.at[pl.ds(r, 1), :], rows_out.at[pl.ds(d, 1), :], sem).start()
        return carry

    lax.fori_loop(0, DISPATCH_TOKENS, token, 0)
    for _ in range(TOP_K):
        pltpu.make_async_copy(x_ref, x_ref, sem).wait()


def _dispatch(dest_flat, x1, n_rows):
    t = x1.shape[0]
    tm = DISPATCH_TOKENS
    return pl.pallas_call(
        _dispatch_body,
        grid=(t // tm,),
        in_specs=[pl.BlockSpec((tm * TOP_K,), lambda i: (i,), memory_space=pltpu.SMEM),
                  pl.BlockSpec((tm, D_MODEL), lambda i: (i, 0)),
                  pl.BlockSpec(memory_space=pl.ANY)],
        out_specs=pl.BlockSpec(memory_space=pl.ANY),
        out_shape=jax.ShapeDtypeStruct((n_rows, D_MODEL), F32),
        scratch_shapes=[pltpu.SemaphoreType.DMA(())],
        input_output_aliases={2: 0},
        compiler_params=_params(("arbitrary",)),
        name="moe_dispatch",
    )(dest_flat, x1, jnp.zeros((n_rows, D_MODEL), F32))


def _moe_body(be_ref, bv_ref, x_ref, wgu_ref, bgu_ref, wd_ref, bd_ref, y_ref, wgu_sc, wd_sc):
    i = pl.program_id(0)
    prev = be_ref[jnp.maximum(i - 1, 0)]

    @pl.when((i == 0) | (be_ref[i] != prev))
    def _():
        wgu_sc[...] = wgu_ref[0, 0].astype(BF16)
        wd_sc[...] = wd_ref[0, 0].astype(BF16)

    @pl.when(bv_ref[i] > 0)
    def _():
        h = jnp.dot(x_ref[...].astype(BF16), wgu_sc[...], preferred_element_type=F32) + bgu_ref[0, 0]
        gate = jnp.minimum(h[:, :D_EXPERT], SWIGLU_LIMIT)
        up = jnp.clip(h[:, D_EXPERT:], -SWIGLU_LIMIT, SWIGLU_LIMIT)
        glu = gate * (1.0 / (1.0 + jnp.exp(-SWIGLU_ALPHA * gate)))
        act = ((up + 1.0) * glu).astype(BF16)
        y_ref[...] = jnp.dot(act, wd_sc[...], preferred_element_type=F32) + bd_ref[0, 0]

    @pl.when(bv_ref[i] == 0)
    def _():
        y_ref[...] = jnp.zeros_like(y_ref)


def _moe(layer, block_expert, block_valid, x_rows, wgu, bgu, wd, bd):
    n_rows = x_rows.shape[0]
    expert = lambda i, be, bv: (layer, be[i], 0, 0)
    grid_spec = pltpu.PrefetchScalarGridSpec(
        num_scalar_prefetch=2,
        grid=(n_rows // MOE_ROWS,),
        in_specs=[pl.BlockSpec((MOE_ROWS, D_MODEL), lambda i, be, bv: (i, 0)),
                  pl.BlockSpec((1, 1, D_MODEL, 2 * D_EXPERT), expert),
                  pl.BlockSpec((1, 1, 1, 2 * D_EXPERT), expert),
                  pl.BlockSpec((1, 1, D_EXPERT, D_MODEL), expert),
                  pl.BlockSpec((1, 1, 1, D_MODEL), expert)],
        out_specs=pl.BlockSpec((MOE_ROWS, D_MODEL), lambda i, be, bv: (i, 0)),
        scratch_shapes=[pltpu.VMEM((D_MODEL, 2 * D_EXPERT), BF16), pltpu.VMEM((D_EXPERT, D_MODEL), BF16)],
    )
    return pl.pallas_call(
        _moe_body,
        grid_spec=grid_spec,
        out_shape=jax.ShapeDtypeStruct((n_rows, D_MODEL), F32),
        compiler_params=_params(("arbitrary",)),
        name="moe_experts",
    )(block_expert, block_valid, x_rows, wgu, bgu, wd, bd)


def _combine_body(x_ref, y0_ref, y1_ref, y2_ref, y3_ref, meta_ref, g_ref, beta_ref, o_ref):
    gates = meta_ref[...]
    ffn = (gates[:, 8:9] * y0_ref[...] + gates[:, 9:10] * y1_ref[...]
           + gates[:, 10:11] * y2_ref[...] + gates[:, 11:12] * y3_ref[...])
    o_ref[...] = _layer_norm(DEEPNORM_ALPHA * x_ref[...] + ffn, g_ref[...], beta_ref[...])


def _combine(x1, ys, meta, g, beta):
    t = x1.shape[0]
    tm = 512
    row = lambda i: (i, 0)
    fix = lambda i: (0, 0)
    return pl.pallas_call(
        _combine_body,
        grid=(t // tm,),
        in_specs=[pl.BlockSpec((tm, D_MODEL), row)] * 5
                 + [pl.BlockSpec((tm, 128), row), pl.BlockSpec((1, D_MODEL), fix),
                    pl.BlockSpec((1, D_MODEL), fix)],
        out_specs=pl.BlockSpec((tm, D_MODEL), row),
        out_shape=jax.ShapeDtypeStruct((t, D_MODEL), F32),
        compiler_params=_params(("parallel",)),
        name="combine_ln",
    )(x1, *ys, meta, g, beta)


def _prep_w_in(w):
    scale = DA ** -0.5
    col = jnp.ones((D_QKV,), F32)
    for lo, width in ((0, HA * 2 * DA), (3 * HA * 2 * DA, HB * DB),
                      (3 * HA * 2 * DA + 3 * HB * DB, HC * DC)):
        col = col.at[lo:lo + width].set(scale)
    main = w[:, :D_QKV] * col
    q_idx = w[:, D_QKV:D_QKV + HI * DI]
    k_idx = w[:, D_QKV + HI * DI:D_QKV + HI * DI + DI]
    w_idx = w[:, D_QKV + HI * DI + DI:] * ((HI ** -0.5) * (DI ** -0.5))
    w_pad = jnp.zeros((D_MODEL, D_W - HI), F32)
    return jnp.concatenate([main, q_idx, jnp.tile(k_idx, (1, HI)), w_idx, w_pad], axis=1).astype(BF16)


def _route(meta, cnt, n_tok):
    top_e = meta[:, 0:TOP_K].astype(I32)
    rank = meta[:, TOP_K:2 * TOP_K].astype(I32)
    counts = cnt[0, :N_EXPERTS].astype(I32)
    n_blocks = n_tok * TOP_K // MOE_ROWS + N_EXPERTS
    blocks_e = (counts + MOE_ROWS - 1) // MOE_ROWS
    blocks_end = jnp.cumsum(blocks_e)
    row_start = (blocks_end - blocks_e) * MOE_ROWS
    experts = jnp.arange(N_EXPERTS, dtype=I32)
    start_of = jnp.sum(jnp.where(top_e[:, :, None] == experts, row_start, 0), axis=-1)
    dest = start_of + rank
    blk = jnp.arange(n_blocks, dtype=I32)
    block_expert = jnp.minimum(jnp.sum((blocks_end[None, :] <= blk[:, None]).astype(I32), axis=1),
                               N_EXPERTS - 1)
    block_valid = (blk < blocks_end[-1]).astype(I32)
    return dest, n_blocks * MOE_ROWS, block_expert, block_valid


def kernel(x, w_in, lam_q1, lam_k1, lam_q2, lam_k2, subln_g, rel_bias, w_out, ln1_g, ln1_b,
           w_router, b_router, w_gu, b_gu, w_down, b_down, ln2_g, ln2_b):
    bsz, seq, _ = x.shape
    n_tok = bsz * seq
    slopes = [2.0 ** (-8.0 * i / (HA + HC)) for i in range(1, HA + HC + 1)]
    slopes_a = tuple(slopes[0::2])
    slopes_c = tuple(slopes[1::2])
    x2d = x.reshape(n_tok, D_MODEL)
    for l in range(DEPTH):
        lam_init = 0.8 - 0.6 * math.exp(-0.3 * l)
        lam = (jnp.exp(jnp.sum(lam_q1[l] * lam_k1[l])) - jnp.exp(jnp.sum(lam_q2[l] * lam_k2[l]))
               + lam_init).reshape(1).astype(F32)
        hm, hik, hw = _in_proj(x2d, _prep_w_in(w_in[l]))
        hm3 = hm.reshape(bsz, seq, D_QKV)
        out_a = _diff_attn(hm3, slopes_a, lam, subln_g[l].reshape(1, 2 * DA), lam_init)
        out_b = _band_attn(hm3, _band_bias(rel_bias[l]))
        out_c = _dsa_attn(hm3, hik.reshape(bsz, seq, D_IK), hw.reshape(bsz, seq, D_W), slopes_c)
        x1, meta, cnt = _out_router(
            out_a.reshape(n_tok, -1), out_b.reshape(n_tok, -1), out_c.reshape(n_tok, -1), x2d,
            w_out[l].astype(BF16), ln1_g[l].reshape(1, -1), ln1_b[l].reshape(1, -1),
            w_router[l], b_router[l].reshape(1, -1))
        dest, n_rows, block_expert, block_valid = _route(meta, cnt, n_tok)
        x_rows = _dispatch(dest.reshape(-1), x1, n_rows)
        y_rows = _moe(l, block_expert, block_valid, x_rows, w_gu, b_gu.reshape(DEPTH, N_EXPERTS, 1, -1),
                      w_down, b_down.reshape(DEPTH, N_EXPERTS, 1, -1))
        ys = [y_rows[dest[:, kk]] for kk in range(TOP_K)]
        x2d = _combine(x1, ys, meta, ln2_g[l].reshape(1, -1), ln2_b[l].reshape(1, -1))
    return x2d.reshape(bsz, seq, D_MODEL)
```

```python
import functools
import math

import jax
import jax.numpy as jnp
from jax import lax
from jax.experimental import pallas as pl
from jax.experimental.pallas import tpu as pltpu

F32 = jnp.float32
BF16 = jnp.bfloat16
I32 = jnp.int32

D_MODEL = 1024
DEPTH = 4
CHUNK = 64
CHUNK_SHIFT = 6
HA, DA = 4, 64
HB, DB = 4, 64
BAND_CHUNKS = 8
REL_CLIP = 128
HC, DC = 4, 64
HI, DI = 8, 32
TOPK_MAX = 256
N_EXPERTS = 32
TOP_K = 4
D_EXPERT = D_MODEL
SWIGLU_LIMIT = 7.0
SWIGLU_ALPHA = 1.702
DEEPNORM_ALPHA = (2 * DEPTH) ** 0.25
EPS = 1e-5
NEG_INF = -1e30

D_QKV = 3 * HA * 2 * DA + 3 * HB * DB + 3 * HC * DC
D_IK = 2 * HI * DI
D_W = 128
D_MIX = HA * 2 * DA + HB * DB + HC * DC

VMEM_LIMIT = 56 * 1024 * 1024

_NT = (((1,), (1,)), ((), ()))
_TN = (((0,), (0,)), ((), ()))


def _params(sem):
    return pltpu.CompilerParams(dimension_semantics=sem, vmem_limit_bytes=VMEM_LIMIT)


def _col_sum32(x):
    n, t = x.shape
    return x.reshape(n // 32, 32, t).sum(axis=0)


def _in_proj_body(x_ref, w_ref, hm_ref, hik_ref, hw_ref):
    xb = x_ref[...].astype(BF16)
    for n0 in range(0, D_QKV, 512):
        hm_ref[:, n0:n0 + 512] = jnp.dot(
            xb, w_ref[:, n0:n0 + 512], preferred_element_type=F32).astype(BF16)
    hik_ref[...] = jnp.dot(xb, w_ref[:, D_QKV:D_QKV + D_IK], preferred_element_type=F32).astype(BF16)
    hw_ref[...] = jnp.dot(xb, w_ref[:, D_QKV + D_IK:], preferred_element_type=F32)


def _in_proj(x2d, w):
    t = x2d.shape[0]
    tm = 512
    dw = D_QKV + D_IK + D_W
    return pl.pallas_call(
        _in_proj_body,
        grid=(t // tm,),
        in_specs=[pl.BlockSpec((tm, D_MODEL), lambda i: (i, 0)),
                  pl.BlockSpec((D_MODEL, dw), lambda i: (0, 0))],
        out_specs=[pl.BlockSpec((tm, D_QKV), lambda i: (i, 0)),
                   pl.BlockSpec((tm, D_IK), lambda i: (i, 0)),
                   pl.BlockSpec((tm, D_W), lambda i: (i, 0))],
        out_shape=[jax.ShapeDtypeStruct((t, D_QKV), BF16),
                   jax.ShapeDtypeStruct((t, D_IK), BF16),
                   jax.ShapeDtypeStruct((t, D_W), F32)],
        compiler_params=_params(("parallel",)),
        name="in_proj",
    )(x2d, w)


def _diff_attn_body(lam_ref, q_ref, k_ref, v_ref, g_ref, o_ref,
                    bias_sc, z_sc, p_sc, *state, t, slopes, out_scale):
    qi = pl.program_id(1)
    lam = lam_ref[0]
    dh = 2 * DA
    lane = lax.broadcasted_iota(I32, (1, dh), 1)
    q_half = []
    for h in range(HA):
        qh = q_ref[0, :, h * dh:(h + 1) * dh]
        zero = jnp.zeros_like(qh)
        q_half.append((jnp.where(lane < DA, qh, zero), jnp.where(lane >= DA, qh, zero)))

    kk = lax.broadcasted_iota(I32, (t, t), 0)
    tt = lax.broadcasted_iota(I32, (t, t), 1)
    own_chunk = (kk >> CHUNK_SHIFT) <= (tt >> CHUNK_SHIFT)
    for h in range(HA):
        bias_sc[h] = slopes[h] * kk.astype(F32)
        bias_sc[HA + h] = jnp.where(own_chunk, slopes[h] * jnp.minimum(kk, 2 * tt - kk).astype(F32), NEG_INF)

    n_chain = 2 * HA
    m_sc, l_sc, acc_sc = state[:n_chain], state[n_chain:2 * n_chain], state[2 * n_chain:]
    for c in range(n_chain):
        m_sc[c][...] = jnp.full((1, t), NEG_INF, F32)
        l_sc[c][...] = jnp.zeros((1, t), F32)
        acc_sc[c][...] = jnp.zeros((dh, t), F32)

    def kv_block(j, bias_base):
        k0 = pl.multiple_of(j * t, t)
        k0f = k0.astype(F32)
        zmax = []
        for h in range(HA):
            kb = k_ref[0, pl.ds(k0, t), h * dh:(h + 1) * dh]
            bias = bias_sc[bias_base + h]
            for a in range(2):
                z = lax.dot_general(kb, q_half[h][a], _NT, preferred_element_type=F32) + bias
                z_sc[2 * h + a] = z
                zmax.append(jnp.max(z, axis=0, keepdims=True))
        alphas = []
        for c in range(n_chain):
            shift = slopes[c // 2] * k0f
            m_old = m_sc[c][...]
            m_new = jnp.maximum(m_old, zmax[c] + shift)
            alpha = jnp.exp(m_old - m_new)
            p = jnp.exp(z_sc[c] - (m_new - shift))
            l_sc[c][...] = alpha * l_sc[c][...] + jnp.sum(p, axis=0, keepdims=True)
            m_sc[c][...] = m_new
            p_sc[c] = p.astype(BF16)
            alphas.append(alpha)
        for h in range(HA):
            vt = v_ref[0, pl.ds(k0, t), h * dh:(h + 1) * dh].T
            for a in range(2):
                c = 2 * h + a
                acc_sc[c][...] = alphas[c] * acc_sc[c][...] + jnp.dot(
                    vt, p_sc[c], preferred_element_type=F32)

    def past_step(j, carry):
        kv_block(j, 0)
        return carry

    lax.fori_loop(0, qi, past_step, 0)
    kv_block(qi, HA)

    outs = []
    for h in range(HA):
        ot = (acc_sc[2 * h][...] / l_sc[2 * h][...]
              - lam * (acc_sc[2 * h + 1][...] / l_sc[2 * h + 1][...]))
        ms = jnp.mean(ot * ot, axis=0, keepdims=True)
        outs.append((ot * lax.rsqrt(ms + EPS)).T * g_ref[...] * out_scale)
    o_ref[0] = jnp.concatenate(outs, axis=1).astype(BF16)


def _diff_attn(hm3, slopes, lam, sub_g, lam_init):
    b, s, _ = hm3.shape
    t = 256
    dh = 2 * DA
    w = HA * dh
    body = functools.partial(_diff_attn_body, t=t, slopes=slopes, out_scale=1.0 - lam_init)
    return pl.pallas_call(
        body,
        grid=(b, s // t),
        in_specs=[pl.BlockSpec(memory_space=pltpu.SMEM),
                  pl.BlockSpec((1, t, w), lambda bi, qi: (bi, qi, 0)),
                  pl.BlockSpec((1, s, w), lambda bi, qi: (bi, 0, 1)),
                  pl.BlockSpec((1, s, w), lambda bi, qi: (bi, 0, 2)),
                  pl.BlockSpec((1, dh), lambda bi, qi: (0, 0))],
        out_specs=pl.BlockSpec((1, t, w), lambda bi, qi: (bi, qi, 0)),
        out_shape=jax.ShapeDtypeStruct((b, s, w), BF16),
        scratch_shapes=([pltpu.VMEM((2 * HA, t, t), F32), pltpu.VMEM((2 * HA, t, t), F32),
                         pltpu.VMEM((2 * HA, t, t), BF16)]
                        + [pltpu.VMEM((1, t), F32)] * (4 * HA) + [pltpu.VMEM((dh, t), F32)] * (2 * HA)),
        compiler_params=_params(("parallel", "arbitrary")),
        name="diff_attn",
    )(lam, hm3, hm3, hm3, sub_g)


BAND_TQ = 256
BAND_TK = 3 * BAND_TQ


def _band_attn_body(q_ref, k0_ref, k1_ref, k2_ref, v0_ref, v1_ref, v2_ref, bias_ref, o_ref):
    i = pl.program_id(1)
    q = q_ref[0]
    k = jnp.concatenate([k0_ref[0], k1_ref[0], k2_ref[0]], axis=0)
    v = jnp.concatenate([v0_ref[0], v1_ref[0], v2_ref[0]], axis=0)
    col = lax.broadcasted_iota(I32, (1, BAND_TK), 1)
    first_valid = jnp.where(i >= 2, 0, jnp.where(i >= 1, BAND_TQ, 2 * BAND_TQ))
    kvalid = col >= first_valid
    lane = lax.broadcasted_iota(I32, (1, HB * DB), 1)
    zero = jnp.zeros_like(q)
    out = jnp.zeros((BAND_TQ, HB * DB), F32)
    for h in range(HB):
        head = (lane >> 6) == h
        s = lax.dot_general(jnp.where(head, q, zero), k, _NT, preferred_element_type=F32)
        s = jnp.where(kvalid, s + bias_ref[h], NEG_INF)
        m = jnp.max(s, axis=-1, keepdims=True)
        p = jnp.exp(s - m)
        l = jnp.sum(p, axis=-1, keepdims=True)
        o = jnp.dot(p.astype(BF16), v, preferred_element_type=F32) / l
        out = jnp.where(head, o, out)
    o_ref[0] = out.astype(BF16)


def _band_bias(rel_bias):
    r = jnp.arange(BAND_TQ)[:, None]
    j = jnp.arange(BAND_TK)[None, :]
    qc = r // CHUNK
    kc = j // CHUNK
    in_band = (kc >= qc) & (kc <= qc + BAND_CHUNKS)
    n = BAND_TQ + BAND_TK
    m = jnp.arange(n)
    delta = jnp.where(m < BAND_TK, m, m - n)
    by_delta = rel_bias.astype(F32)[:, jnp.clip(2 * BAND_TQ - delta, -REL_CLIP, REL_CLIP) + REL_CLIP]
    flat = jnp.tile(by_delta, (1, BAND_TQ))[:, :BAND_TQ * (n - 1)]
    bias = flat.reshape(HB, BAND_TQ, n - 1)[:, :, :BAND_TK]
    return jnp.where(in_band[None], bias, NEG_INF)


def _band_attn(hm3, bias):
    b, s, _ = hm3.shape
    w = HB * DB
    cq, ck, cv = 6, 7, 8

    def kv_spec(col, back):
        return pl.BlockSpec((1, BAND_TQ, w), lambda bi, i: (bi, jnp.maximum(i - back, 0), col))

    return pl.pallas_call(
        _band_attn_body,
        grid=(b, s // BAND_TQ),
        in_specs=[pl.BlockSpec((1, BAND_TQ, w), lambda bi, i: (bi, i, cq)),
                  kv_spec(ck, 2), kv_spec(ck, 1), kv_spec(ck, 0),
                  kv_spec(cv, 2), kv_spec(cv, 1), kv_spec(cv, 0),
                  pl.BlockSpec((HB, BAND_TQ, BAND_TK), lambda bi, i: (0, 0, 0))],
        out_specs=pl.BlockSpec((1, BAND_TQ, w), lambda bi, i: (bi, i, 0)),
        out_shape=jax.ShapeDtypeStruct((b, s, w), BF16),
        compiler_params=_params(("parallel", "parallel")),
        name="band_attn",
    )(hm3, hm3, hm3, hm3, hm3, hm3, hm3, bias)


INT_MIN = -2 ** 31
NEG_INF_KEY = -1900671691
UNSELECTED = -1e33
DSA_T = 256


def _order_key(x):
    bits = lax.bitcast_convert_type(x, I32)
    return jnp.where(bits < 0, bits ^ jnp.int32(0x7FFFFFFF), bits)


def _dsa_body(q_ref, k_ref, v_ref, qi_ref, ki_ref, w_ref, o_ref, key_sc, base_sc, z_sc, p_sc, *state,
              n_sel, slopes):
    t = DSA_T
    qi = pl.program_id(1)
    q0 = qi * t
    nblk = qi + 1
    kk = lax.broadcasted_iota(I32, (t, t), 0)
    tg = q0 + lax.broadcasted_iota(I32, (t, t), 1)

    def block_start(j):
        return pl.multiple_of(j * t, t)

    qidx = qi_ref[0]
    lane_i = lax.broadcasted_iota(I32, (1, HI * DI), 1)
    zero_i = jnp.zeros_like(qidx)
    qm = [jnp.where((lane_i >> 5) == h, qidx, zero_i) for h in range(HI)]
    wt = w_ref[0].T

    def score_step(j, carry):
        k0 = block_start(j)
        kib = ki_ref[0, pl.ds(k0, t), :]
        sc = jnp.zeros((t, t), F32)
        for h in range(HI):
            rel = jnp.maximum(lax.dot_general(kib, qm[h], _NT, preferred_element_type=F32), 0.0)
            sc = sc + wt[h:h + 1, :] * rel
        allowed = ((k0 + kk) >> CHUNK_SHIFT) <= (tg >> CHUNK_SHIFT)
        key_sc[pl.ds(k0, t), :] = _order_key(jnp.where(allowed, sc, NEG_INF))
        return carry

    lax.fori_loop(0, nblk, score_step, 0)

    kf = float(n_sel)

    def count_ge(cand):
        def blk(j, acc):
            kb = key_sc[pl.ds(block_start(j), t), :]
            return acc + _col_sum32(jnp.where(kb >= cand, 1.0, 0.0))
        acc = lax.fori_loop(0, nblk, blk, jnp.zeros((32, t), F32))
        return jnp.sum(acc, axis=0, keepdims=True)

    thr = jnp.where(count_ge(jnp.zeros((1, t), I32)) >= kf, 0, INT_MIN).astype(I32)

    def bit_step(i, thr):
        cand = thr + jnp.left_shift(jnp.int32(1), 30 - i)
        return jnp.where(count_ge(cand) >= kf, cand, thr)

    thr = lax.fori_loop(0, 31, bit_step, thr)

    def alibi_or_unselected(keep, k0):
        kg = k0 + kk
        allowed = (kg >> CHUNK_SHIFT) <= (tg >> CHUNK_SHIFT)
        return jnp.where(keep, jnp.where(allowed, -jnp.abs(tg - kg).astype(F32), UNSELECTED), UNSELECTED)

    def select_step(j, n_ge):
        k0 = block_start(j)
        ge = key_sc[pl.ds(k0, t), :] >= thr
        base_sc[pl.ds(k0, t), :] = alibi_or_unselected(ge, k0)
        return n_ge + _col_sum32(jnp.where(ge, 1.0, 0.0))

    n_ge = jnp.sum(lax.fori_loop(0, nblk, select_step, jnp.zeros((32, t), F32)), axis=0, keepdims=True)
    over = jnp.where(n_ge > kf, jnp.where(thr != NEG_INF_KEY, 1.0, 0.0), 0.0)

    @pl.when(jnp.max(over) > 0.0)
    def _():
        def gt_step(j, acc):
            kb = key_sc[pl.ds(block_start(j), t), :]
            return acc + _col_sum32(jnp.where(kb > thr, 1.0, 0.0))
        n_gt = jnp.sum(lax.fori_loop(0, nblk, gt_step, jnp.zeros((32, t), F32)), axis=0, keepdims=True)
        need = kf - n_gt
        earlier = jnp.where(lax.broadcasted_iota(I32, (t, t), 1) < kk, 1.0, 0.0).astype(BF16)

        def tie_step(j, offs):
            k0 = block_start(j)
            kb = key_sc[pl.ds(k0, t), :]
            e = jnp.where(kb == thr, 1.0, 0.0)
            rank = jnp.dot(earlier, e.astype(BF16), preferred_element_type=F32) + offs
            keep = jnp.where(kb > thr, 1.0, jnp.where(rank < need, e, 0.0))
            base_sc[pl.ds(k0, t), :] = alibi_or_unselected(keep > 0.0, k0)
            return offs + jnp.sum(e, axis=0, keepdims=True)

        lax.fori_loop(0, nblk, tie_step, jnp.zeros((1, t), F32))

    q = q_ref[0]
    lane = lax.broadcasted_iota(I32, (1, HC * DC), 1)
    zero = jnp.zeros_like(q)
    q_head = [jnp.where((lane >> 6) == h, q, zero) for h in range(HC)]
    m_sc, l_sc, acc_sc = state[:HC], state[HC:2 * HC], state[2 * HC:]
    for h in range(HC):
        m_sc[h][...] = jnp.full((1, t), NEG_INF, F32)
        l_sc[h][...] = jnp.zeros((1, t), F32)
        acc_sc[h][...] = jnp.zeros((DC, t), F32)

    def att_step(j, carry):
        k0 = block_start(j)
        kb = k_ref[0, pl.ds(k0, t), :]
        base = base_sc[pl.ds(k0, t), :]
        zmax = []
        for h in range(HC):
            z = lax.dot_general(kb, q_head[h], _NT, preferred_element_type=F32) + slopes[h] * base
            z_sc[h] = z
            zmax.append(jnp.max(z, axis=0, keepdims=True))
        alphas = []
        for h in range(HC):
            m_old = m_sc[h][...]
            m_new = jnp.maximum(m_old, zmax[h])
            alpha = jnp.exp(m_old - m_new)
            p = jnp.exp(z_sc[h] - m_new)
            l_sc[h][...] = alpha * l_sc[h][...] + jnp.sum(p, axis=0, keepdims=True)
            m_sc[h][...] = m_new
            p_sc[h] = p.astype(BF16)
            alphas.append(alpha)
        vt = v_ref[0, pl.ds(k0, t), :].T
        for h in range(HC):
            acc_sc[h][...] = alphas[h] * acc_sc[h][...] + jnp.dot(
                vt[h * DC:(h + 1) * DC, :], p_sc[h], preferred_element_type=F32)
        return carry

    lax.fori_loop(0, nblk, att_step, 0)
    heads = [acc_sc[h][...] / l_sc[h][...] for h in range(HC)]
    o_ref[0] = jnp.concatenate(heads, axis=0).T.astype(BF16)


def _dsa_attn(hm3, hik3, hw3, slopes):
    b, s, _ = hm3.shape
    t = DSA_T
    w = HC * DC
    cq, ck, cv = 9, 10, 11
    body = functools.partial(_dsa_body, n_sel=min(TOPK_MAX, s // 4), slopes=slopes)
    return pl.pallas_call(
        body,
        grid=(b, s // t),
        in_specs=[pl.BlockSpec((1, t, w), lambda bi, i: (bi, i, cq)),
                  pl.BlockSpec((1, s, w), lambda bi, i: (bi, 0, ck)),
                  pl.BlockSpec((1, s, w), lambda bi, i: (bi, 0, cv)),
                  pl.BlockSpec((1, t, HI * DI), lambda bi, i: (bi, i, 0)),
                  pl.BlockSpec((1, s, HI * DI), lambda bi, i: (bi, 0, 1)),
                  pl.BlockSpec((1, t, D_W), lambda bi, i: (bi, i, 0))],
        out_specs=pl.BlockSpec((1, t, w), lambda bi, i: (bi, i, 0)),
        out_shape=jax.ShapeDtypeStruct((b, s, w), BF16),
        scratch_shapes=([pltpu.VMEM((s, t), I32), pltpu.VMEM((s, t), F32),
                         pltpu.VMEM((HC, t, t), F32), pltpu.VMEM((HC, t, t), BF16)]
                        + [pltpu.VMEM((1, t), F32)] * (2 * HC) + [pltpu.VMEM((DC, t), F32)] * HC),
        compiler_params=_params(("parallel", "arbitrary")),
        name="dsa_attn",
    )(hm3, hm3, hm3, hik3, hik3, hw3)


def _layer_norm(y, g, b):
    mu = jnp.mean(y, axis=-1, keepdims=True)
    d = y - mu
    var = jnp.mean(d * d, axis=-1, keepdims=True)
    return d * lax.rsqrt(var + EPS) * g + b


def _out_router_body(a_ref, b_ref, c_ref, x_ref, wo_ref, g_ref, beta_ref, wr_ref, br_ref,
                     x1_ref, meta_ref, cnt_ref, *, tm):
    i = pl.program_id(0)
    wa = HA * 2 * DA
    wb = wa + HB * DB
    y = (jnp.dot(a_ref[...], wo_ref[0:wa, :], preferred_element_type=F32)
         + jnp.dot(b_ref[...], wo_ref[wa:wb, :], preferred_element_type=F32)
         + jnp.dot(c_ref[...], wo_ref[wb:, :], preferred_element_type=F32))
    x1 = _layer_norm(DEEPNORM_ALPHA * x_ref[...] + y, g_ref[...], beta_ref[...])
    x1_ref[...] = x1

    logits = lax.dot_general(wr_ref[...], x1, _NT, preferred_element_type=F32,
                             precision=lax.Precision.HIGHEST) + br_ref[...]
    row_e = lax.broadcasted_iota(I32, (N_EXPERTS, tm), 0).astype(F32)
    sel = jnp.zeros((N_EXPERTS, tm), F32)
    work = logits
    picks, vals = [], []
    for _ in range(TOP_K):
        v = jnp.max(work, axis=0, keepdims=True)
        e = jnp.min(jnp.where(work == v, row_e, float(N_EXPERTS)), axis=0, keepdims=True)
        hit = row_e == e
        sel = jnp.where(hit, 1.0, sel)
        work = jnp.where(hit, -jnp.inf, work)
        picks.append(e)
        vals.append(v)
    ex = [jnp.exp(v - vals[0]) for v in vals]
    den = ex[0] + ex[1] + ex[2] + ex[3]

    @pl.when(i == 0)
    def _():
        cnt_ref[...] = jnp.zeros_like(cnt_ref)

    earlier = jnp.where(lax.broadcasted_iota(I32, (tm, tm), 0) < lax.broadcasted_iota(I32, (tm, tm), 1),
                        1.0, 0.0).astype(BF16)
    so_far = cnt_ref[...]
    before = jnp.dot(sel.astype(BF16), earlier, preferred_element_type=F32) + so_far[:, 0:1]
    cnt_ref[...] = so_far + jnp.sum(sel, axis=1, keepdims=True)

    rows = lax.broadcasted_iota(I32, (4 * TOP_K, tm), 0)
    meta_t = jnp.zeros((4 * TOP_K, tm), F32)
    for kk in range(TOP_K):
        rank = jnp.sum(jnp.where(row_e == picks[kk], before, 0.0), axis=0, keepdims=True)
        meta_t = jnp.where(rows == kk, picks[kk], meta_t)
        meta_t = jnp.where(rows == TOP_K + kk, rank, meta_t)
        meta_t = jnp.where(rows == 2 * TOP_K + kk, ex[kk] / den, meta_t)
    meta_ref[...] = jnp.concatenate([meta_t, jnp.zeros((128 - 4 * TOP_K, tm), F32)], axis=0).T


def _out_router(oa, ob, oc, x2d, wo, g, beta, wr, br):
    t = x2d.shape[0]
    tm = 512
    row = lambda i: (i, 0)
    fix = lambda i: (0, 0)
    body = functools.partial(_out_router_body, tm=tm)
    return pl.pallas_call(
        body,
        grid=(t // tm,),
        in_specs=[pl.BlockSpec((tm, oa.shape[1]), row), pl.BlockSpec((tm, ob.shape[1]), row),
                  pl.BlockSpec((tm, oc.shape[1]), row), pl.BlockSpec((tm, D_MODEL), row),
                  pl.BlockSpec((D_MIX, D_MODEL), fix), pl.BlockSpec((1, D_MODEL), fix),
                  pl.BlockSpec((1, D_MODEL), fix), pl.BlockSpec((N_EXPERTS, D_MODEL), fix),
                  pl.BlockSpec((N_EXPERTS, 1), fix)],
        out_specs=[pl.BlockSpec((tm, D_MODEL), row),
                   pl.BlockSpec((tm, 128), row), pl.BlockSpec((N_EXPERTS, 128), fix)],
        out_shape=[jax.ShapeDtypeStruct((t, D_MODEL), F32),
                   jax.ShapeDtypeStruct((t, 128), F32), jax.ShapeDtypeStruct((N_EXPERTS, 128), F32)],
        compiler_params=_params(("arbitrary",)),
        name="out_router",
    )(oa, ob, oc, x2d, wo, g, beta, wr, br)


MOE_ROWS = 512
DISPATCH_TOKENS = 256


def _dispatch_body(dest_ref, pad_ref, x_ref, rows_out, zero_sc, sem, zero_sem):
    @pl.when(pl.program_id(0) == 0)
    def _():
        zero_sc[...] = jnp.zeros_like(zero_sc)
        n_blocks = pad_ref.shape[0]

        def zero_copy(b):
            start = pl.multiple_of(b * MOE_ROWS, MOE_ROWS)
            return pltpu.make_async_copy(zero_sc, rows_out.at[pl.ds(start, MOE_ROWS), :], zero_sem)

        def zero_start(b, carry):
            @pl.when(pad_ref[b] > 0)
            def _():
                zero_copy(b).start()
            return carry

        def zero_wait(b, carry):
            @pl.when(pad_ref[b] > 0)
            def _():
                zero_copy(b).wait()
            return carry

        lax.fori_loop(0, n_blocks, zero_start, 0)
        lax.fori_loop(0, n_blocks, zero_wait, 0)

    def token(r, carry):
        for kk in range(TOP_K):
            d = dest_ref[r * TOP_K + kk]
            pltpu.make_async_copy(x_ref.at[pl.ds(r, 1), :], rows_out.at[pl.ds(d, 1), :], sem).start()
        return carry

    lax.fori_loop(0, DISPATCH_TOKENS, token, 0)
    for _ in range(TOP_K):
        pltpu.make_async_copy(x_ref, x_ref, sem).wait()


def _dispatch(dest_flat, block_padded, x1, n_rows):
    t = x1.shape[0]
    tm = DISPATCH_TOKENS
    return pl.pallas_call(
        _dispatch_body,
        grid=(t // tm,),
        in_specs=[pl.BlockSpec((tm * TOP_K,), lambda i: (i,), memory_space=pltpu.SMEM),
                  pl.BlockSpec(memory_space=pltpu.SMEM),
                  pl.BlockSpec((tm, D_MODEL), lambda i: (i, 0))],
        out_specs=pl.BlockSpec(memory_space=pl.ANY),
        out_shape=jax.ShapeDtypeStruct((n_rows, D_MODEL), F32),
        scratch_shapes=[pltpu.VMEM((MOE_ROWS, D_MODEL), F32), pltpu.SemaphoreType.DMA(()),
                        pltpu.SemaphoreType.DMA(())],
        compiler_params=_params(("arbitrary",)),
        name="moe_dispatch",
    )(dest_flat, block_padded, x1)


def _moe_body(be_ref, bv_ref, src_ref, x_ref, wgu_ref, bgu_ref, wd_ref, bd_ref, y_ref, wgu_sc, wd_sc):
    del src_ref
    i = pl.program_id(0)
    prev = be_ref[jnp.maximum(i - 1, 0)]

    @pl.when((i == 0) | (be_ref[i] != prev))
    def _():
        wgu_sc[...] = wgu_ref[0, 0].astype(BF16)
        wd_sc[...] = wd_ref[0, 0].astype(BF16)

    @pl.when(bv_ref[i] > 0)
    def _():
        h = jnp.dot(x_ref[...].astype(BF16), wgu_sc[...], preferred_element_type=F32) + bgu_ref[0, 0]
        gate = jnp.minimum(h[:, :D_EXPERT], SWIGLU_LIMIT)
        up = jnp.clip(h[:, D_EXPERT:], -SWIGLU_LIMIT, SWIGLU_LIMIT)
        glu = gate * (1.0 / (1.0 + jnp.exp(-SWIGLU_ALPHA * gate)))
        act = ((up + 1.0) * glu).astype(BF16)
        y_ref[...] = jnp.dot(act, wd_sc[...], preferred_element_type=F32) + bd_ref[0, 0]

    @pl.when(bv_ref[i] == 0)
    def _():
        y_ref[...] = jnp.zeros_like(y_ref)


def _moe(layer, block_expert, block_valid, block_src, x_rows, wgu, bgu, wd, bd):
    n_rows = x_rows.shape[0]
    expert = lambda i, be, bv, src: (layer, be[i], 0, 0)
    grid_spec = pltpu.PrefetchScalarGridSpec(
        num_scalar_prefetch=3,
        grid=(n_rows // MOE_ROWS,),
        in_specs=[pl.BlockSpec((MOE_ROWS, D_MODEL), lambda i, be, bv, src: (src[i], 0)),
                  pl.BlockSpec((1, 1, D_MODEL, 2 * D_EXPERT), expert),
                  pl.BlockSpec((1, 1, 1, 2 * D_EXPERT), expert),
                  pl.BlockSpec((1, 1, D_EXPERT, D_MODEL), expert),
                  pl.BlockSpec((1, 1, 1, D_MODEL), expert)],
        out_specs=pl.BlockSpec((MOE_ROWS, D_MODEL), lambda i, be, bv, src: (i, 0)),
        scratch_shapes=[pltpu.VMEM((D_MODEL, 2 * D_EXPERT), BF16), pltpu.VMEM((D_EXPERT, D_MODEL), BF16)],
    )
    return pl.pallas_call(
        _moe_body,
        grid_spec=grid_spec,
        out_shape=jax.ShapeDtypeStruct((n_rows, D_MODEL), F32),
        compiler_params=_params(("arbitrary",)),
        name="moe_experts",
    )(block_expert, block_valid, block_src, x_rows, wgu, bgu, wd, bd)


def _combine_body(x_ref, y0_ref, y1_ref, y2_ref, y3_ref, meta_ref, g_ref, beta_ref, o_ref):
    gates = meta_ref[...]
    ffn = (gates[:, 8:9] * y0_ref[...] + gates[:, 9:10] * y1_ref[...]
           + gates[:, 10:11] * y2_ref[...] + gates[:, 11:12] * y3_ref[...])
    o_ref[...] = _layer_norm(DEEPNORM_ALPHA * x_ref[...] + ffn, g_ref[...], beta_ref[...])


def _combine(x1, ys, meta, g, beta):
    t = x1.shape[0]
    tm = 512
    row = lambda i: (i, 0)
    fix = lambda i: (0, 0)
    return pl.pallas_call(
        _combine_body,
        grid=(t // tm,),
        in_specs=[pl.BlockSpec((tm, D_MODEL), row)] * 5
                 + [pl.BlockSpec((tm, 128), row), pl.BlockSpec((1, D_MODEL), fix),
                    pl.BlockSpec((1, D_MODEL), fix)],
        out_specs=pl.BlockSpec((tm, D_MODEL), row),
        out_shape=jax.ShapeDtypeStruct((t, D_MODEL), F32),
        compiler_params=_params(("parallel",)),
        name="combine_ln",
    )(x1, *ys, meta, g, beta)


def _prep_w_in(w):
    scale = DA ** -0.5
    col = jnp.ones((D_QKV,), F32)
    for lo, width in ((0, HA * 2 * DA), (3 * HA * 2 * DA, HB * DB),
                      (3 * HA * 2 * DA + 3 * HB * DB, HC * DC)):
        col = col.at[lo:lo + width].set(scale)
    main = w[:, :D_QKV] * col
    q_idx = w[:, D_QKV:D_QKV + HI * DI]
    k_idx = w[:, D_QKV + HI * DI:D_QKV + HI * DI + DI]
    w_idx = w[:, D_QKV + HI * DI + DI:] * ((HI ** -0.5) * (DI ** -0.5))
    w_pad = jnp.zeros((D_MODEL, D_W - HI), F32)
    return jnp.concatenate([main, q_idx, jnp.tile(k_idx, (1, HI)), w_idx, w_pad], axis=1).astype(BF16)


def _route(meta, cnt, n_tok):
    top_e = meta[:, 0:TOP_K].astype(I32)
    rank = meta[:, TOP_K:2 * TOP_K].astype(I32)
    counts = cnt[:, 0].astype(I32)
    n_blocks = n_tok * TOP_K // MOE_ROWS + N_EXPERTS
    blocks_e = (counts + MOE_ROWS - 1) // MOE_ROWS
    blocks_end = jnp.cumsum(blocks_e)
    row_start = (blocks_end - blocks_e) * MOE_ROWS
    experts = jnp.arange(N_EXPERTS, dtype=I32)
    start_of = jnp.sum(jnp.where(top_e[:, :, None] == experts, row_start, 0), axis=-1)
    dest = start_of + rank
    blk = jnp.arange(n_blocks, dtype=I32)
    block_expert = jnp.minimum(jnp.sum((blocks_end[None, :] <= blk[:, None]).astype(I32), axis=1),
                               N_EXPERTS - 1)
    block_valid = (blk < blocks_end[-1]).astype(I32)
    block_src = jnp.minimum(blk, blocks_end[-1] - 1)
    is_last = jnp.any((blocks_end[None, :] == blk[:, None] + 1) & (blocks_e[None, :] > 0), axis=1)
    block_padded = jnp.where(block_valid > 0, is_last, True).astype(I32)
    return dest, block_padded, n_blocks * MOE_ROWS, block_expert, block_valid, block_src


def kernel(x, w_in, lam_q1, lam_k1, lam_q2, lam_k2, subln_g, rel_bias, w_out, ln1_g, ln1_b,
           w_router, b_router, w_gu, b_gu, w_down, b_down, ln2_g, ln2_b):
    bsz, seq, _ = x.shape
    n_tok = bsz * seq
    slopes = [2.0 ** (-8.0 * i / (HA + HC)) for i in range(1, HA + HC + 1)]
    slopes_a = tuple(slopes[0::2])
    slopes_c = tuple(slopes[1::2])
    x2d = x.reshape(n_tok, D_MODEL)
    for l in range(DEPTH):
        lam_init = 0.8 - 0.6 * math.exp(-0.3 * l)
        lam = (jnp.exp(jnp.sum(lam_q1[l] * lam_k1[l])) - jnp.exp(jnp.sum(lam_q2[l] * lam_k2[l]))
               + lam_init).reshape(1).astype(F32)
        hm, hik, hw = _in_proj(x2d, _prep_w_in(w_in[l]))
        hm3 = hm.reshape(bsz, seq, D_QKV)
        out_a = _diff_attn(hm3, slopes_a, lam, subln_g[l].reshape(1, 2 * DA), lam_init)
        out_b = _band_attn(hm3, _band_bias(rel_bias[l]))
        out_c = _dsa_attn(hm3, hik.reshape(bsz, seq, D_IK), hw.reshape(bsz, seq, D_W), slopes_c)
        x1, meta, cnt = _out_router(
            out_a.reshape(n_tok, -1), out_b.reshape(n_tok, -1), out_c.reshape(n_tok, -1), x2d,
            w_out[l].astype(BF16), ln1_g[l].reshape(1, -1), ln1_b[l].reshape(1, -1),
            w_router[l].T, b_router[l].reshape(-1, 1))
        dest, block_padded, n_rows, block_expert, block_valid, block_src = _route(meta, cnt, n_tok)
        x_rows = _dispatch(dest.reshape(-1), block_padded, x1, n_rows)
        y_rows = _moe(l, block_expert, block_valid, block_src, x_rows, w_gu,
                      b_gu.reshape(DEPTH, N_EXPERTS, 1, -1), w_down, b_down.reshape(DEPTH, N_EXPERTS, 1, -1))
        ys = [y_rows[dest[:, kk]] for kk in range(TOP_K)]
        x2d = _combine(x1, ys, meta, ln2_g[l].reshape(1, -1), ln2_b[l].reshape(1, -1))
    return x2d.reshape(bsz, seq, D_MODEL)
```

```python
import functools
import math

import jax
import jax.numpy as jnp
from jax import lax
from jax.experimental import pallas as pl
from jax.experimental.pallas import tpu as pltpu

F32 = jnp.float32
BF16 = jnp.bfloat16
I32 = jnp.int32
I16 = jnp.int16

D_MODEL = 1024
DEPTH = 4
CHUNK = 64
CHUNK_SHIFT = 6
HA, DA = 4, 64
HB, DB = 4, 64
BAND_CHUNKS = 8
REL_CLIP = 128
HC, DC = 4, 64
HI, DI = 8, 32
TOPK_MAX = 256
N_EXPERTS = 32
TOP_K = 4
D_EXPERT = D_MODEL
SWIGLU_LIMIT = 7.0
SWIGLU_ALPHA = 1.702
DEEPNORM_ALPHA = (2 * DEPTH) ** 0.25
EPS = 1e-5
NEG_INF = -1e30

D_QKV = 3 * HA * 2 * DA + 3 * HB * DB + 3 * HC * DC
D_IK = 2 * HI * DI
D_W = 128
D_MIX = HA * 2 * DA + HB * DB + HC * DC

VMEM_LIMIT = 56 * 1024 * 1024

_NT = (((1,), (1,)), ((), ()))
_TN = (((0,), (0,)), ((), ()))


def _params(sem):
    return pltpu.CompilerParams(dimension_semantics=sem, vmem_limit_bytes=VMEM_LIMIT)


def _col_sum32(x):
    n, t = x.shape
    return x.reshape(n // 32, 32, t).sum(axis=0)


def _in_proj_body(x_ref, w_ref, hm_ref, hik_ref, hw_ref):
    xb = x_ref[...].astype(BF16)
    for n0 in range(0, D_QKV, 512):
        hm_ref[:, n0:n0 + 512] = jnp.dot(
            xb, w_ref[:, n0:n0 + 512], preferred_element_type=F32).astype(BF16)
    hik_ref[...] = jnp.dot(xb, w_ref[:, D_QKV:D_QKV + D_IK], preferred_element_type=F32).astype(BF16)
    hw_ref[...] = jnp.dot(xb, w_ref[:, D_QKV + D_IK:], preferred_element_type=F32)


def _in_proj(x2d, w):
    t = x2d.shape[0]
    tm = 512
    dw = D_QKV + D_IK + D_W
    return pl.pallas_call(
        _in_proj_body,
        grid=(t // tm,),
        in_specs=[pl.BlockSpec((tm, D_MODEL), lambda i: (i, 0)),
                  pl.BlockSpec((D_MODEL, dw), lambda i: (0, 0))],
        out_specs=[pl.BlockSpec((tm, D_QKV), lambda i: (i, 0)),
                   pl.BlockSpec((tm, D_IK), lambda i: (i, 0)),
                   pl.BlockSpec((tm, D_W), lambda i: (i, 0))],
        out_shape=[jax.ShapeDtypeStruct((t, D_QKV), BF16),
                   jax.ShapeDtypeStruct((t, D_IK), BF16),
                   jax.ShapeDtypeStruct((t, D_W), F32)],
        compiler_params=_params(("parallel",)),
        name="in_proj",
    )(x2d, w)


def _diff_attn_body(lam_ref, q_ref, k_ref, v_ref, g_ref, o_ref,
                    bias_sc, z_sc, p_sc, *state, t, slopes, out_scale):
    qi = pl.program_id(1)
    lam = lam_ref[0]
    dh = 2 * DA
    lane = lax.broadcasted_iota(I32, (1, dh), 1)
    q_half = []
    for h in range(HA):
        qh = q_ref[0, :, h * dh:(h + 1) * dh]
        zero = jnp.zeros_like(qh)
        q_half.append((jnp.where(lane < DA, qh, zero), jnp.where(lane >= DA, qh, zero)))

    kk = lax.broadcasted_iota(I32, (t, t), 0)
    tt = lax.broadcasted_iota(I32, (t, t), 1)
    own_chunk = (kk >> CHUNK_SHIFT) <= (tt >> CHUNK_SHIFT)
    for h in range(HA):
        bias_sc[h] = slopes[h] * kk.astype(F32)
        bias_sc[HA + h] = jnp.where(own_chunk, slopes[h] * jnp.minimum(kk, 2 * tt - kk).astype(F32), NEG_INF)

    n_chain = 2 * HA
    m_sc, l_sc, acc_sc = state[:n_chain], state[n_chain:2 * n_chain], state[2 * n_chain:]
    for c in range(n_chain):
        m_sc[c][...] = jnp.full((1, t), NEG_INF, F32)
        l_sc[c][...] = jnp.zeros((1, t), F32)
        acc_sc[c][...] = jnp.zeros((dh, t), F32)

    def kv_block(j, bias_base):
        k0 = pl.multiple_of(j * t, t)
        k0f = k0.astype(F32)
        zmax = []
        for h in range(HA):
            kb = k_ref[0, pl.ds(k0, t), h * dh:(h + 1) * dh]
            bias = bias_sc[bias_base + h]
            for a in range(2):
                z = lax.dot_general(kb, q_half[h][a], _NT, preferred_element_type=F32) + bias
                z_sc[2 * h + a] = z
                zmax.append(jnp.max(z, axis=0, keepdims=True))
        alphas = []
        for c in range(n_chain):
            shift = slopes[c // 2] * k0f
            m_old = m_sc[c][...]
            m_new = jnp.maximum(m_old, zmax[c] + shift)
            alpha = jnp.exp(m_old - m_new)
            p = jnp.exp(z_sc[c] - (m_new - shift))
            l_sc[c][...] = alpha * l_sc[c][...] + jnp.sum(p, axis=0, keepdims=True)
            m_sc[c][...] = m_new
            p_sc[c] = p.astype(BF16)
            alphas.append(alpha)
        for h in range(HA):
            vt = v_ref[0, pl.ds(k0, t), h * dh:(h + 1) * dh].T
            for a in range(2):
                c = 2 * h + a
                acc_sc[c][...] = alphas[c] * acc_sc[c][...] + jnp.dot(
                    vt, p_sc[c], preferred_element_type=F32)

    def past_step(j, carry):
        kv_block(j, 0)
        return carry

    lax.fori_loop(0, qi, past_step, 0)
    kv_block(qi, HA)

    outs = []
    for h in range(HA):
        ot = (acc_sc[2 * h][...] / l_sc[2 * h][...]
              - lam * (acc_sc[2 * h + 1][...] / l_sc[2 * h + 1][...]))
        ms = jnp.mean(ot * ot, axis=0, keepdims=True)
        outs.append((ot * lax.rsqrt(ms + EPS)).T * g_ref[...] * out_scale)
    o_ref[0] = jnp.concatenate(outs, axis=1).astype(BF16)


def _diff_attn(hm3, slopes, lam, sub_g, lam_init):
    b, s, _ = hm3.shape
    t = 256
    dh = 2 * DA
    w = HA * dh
    body = functools.partial(_diff_attn_body, t=t, slopes=slopes, out_scale=1.0 - lam_init)
    return pl.pallas_call(
        body,
        grid=(b, s // t),
        in_specs=[pl.BlockSpec(memory_space=pltpu.SMEM),
                  pl.BlockSpec((1, t, w), lambda bi, qi: (bi, qi, 0)),
                  pl.BlockSpec((1, s, w), lambda bi, qi: (bi, 0, 1)),
                  pl.BlockSpec((1, s, w), lambda bi, qi: (bi, 0, 2)),
                  pl.BlockSpec((1, dh), lambda bi, qi: (0, 0))],
        out_specs=pl.BlockSpec((1, t, w), lambda bi, qi: (bi, qi, 0)),
        out_shape=jax.ShapeDtypeStruct((b, s, w), BF16),
        scratch_shapes=([pltpu.VMEM((2 * HA, t, t), F32), pltpu.VMEM((2 * HA, t, t), F32),
                         pltpu.VMEM((2 * HA, t, t), BF16)]
                        + [pltpu.VMEM((1, t), F32)] * (4 * HA) + [pltpu.VMEM((dh, t), F32)] * (2 * HA)),
        compiler_params=_params(("parallel", "arbitrary")),
        name="diff_attn",
    )(lam, hm3, hm3, hm3, sub_g)


BAND_TQ = 256
BAND_TK = 3 * BAND_TQ


def _band_attn_body(q_ref, k0_ref, k1_ref, k2_ref, v0_ref, v1_ref, v2_ref, bias_ref, o_ref):
    i = pl.program_id(1)
    q = q_ref[0]
    k = jnp.concatenate([k0_ref[0], k1_ref[0], k2_ref[0]], axis=0)
    v = jnp.concatenate([v0_ref[0], v1_ref[0], v2_ref[0]], axis=0)
    col = lax.broadcasted_iota(I32, (1, BAND_TK), 1)
    first_valid = jnp.where(i >= 2, 0, jnp.where(i >= 1, BAND_TQ, 2 * BAND_TQ))
    kvalid = col >= first_valid
    lane = lax.broadcasted_iota(I32, (1, HB * DB), 1)
    zero = jnp.zeros_like(q)
    out = jnp.zeros((BAND_TQ, HB * DB), F32)
    for h in range(HB):
        head = (lane >> 6) == h
        s = lax.dot_general(jnp.where(head, q, zero), k, _NT, preferred_element_type=F32)
        s = jnp.where(kvalid, s + bias_ref[h], NEG_INF)
        m = jnp.max(s, axis=-1, keepdims=True)
        p = jnp.exp(s - m)
        l = jnp.sum(p, axis=-1, keepdims=True)
        o = jnp.dot(p.astype(BF16), v, preferred_element_type=F32) / l
        out = jnp.where(head, o, out)
    o_ref[0] = out.astype(BF16)


def _band_bias(rel_bias):
    r = jnp.arange(BAND_TQ)[:, None]
    j = jnp.arange(BAND_TK)[None, :]
    qc = r // CHUNK
    kc = j // CHUNK
    in_band = (kc >= qc) & (kc <= qc + BAND_CHUNKS)
    n = BAND_TQ + BAND_TK
    m = jnp.arange(n)
    delta = jnp.where(m < BAND_TK, m, m - n)
    by_delta = rel_bias.astype(F32)[:, jnp.clip(2 * BAND_TQ - delta, -REL_CLIP, REL_CLIP) + REL_CLIP]
    flat = jnp.tile(by_delta, (1, BAND_TQ))[:, :BAND_TQ * (n - 1)]
    bias = flat.reshape(HB, BAND_TQ, n - 1)[:, :, :BAND_TK]
    return jnp.where(in_band[None], bias, NEG_INF)


def _band_attn(hm3, bias):
    b, s, _ = hm3.shape
    w = HB * DB
    cq, ck, cv = 6, 7, 8

    def kv_spec(col, back):
        return pl.BlockSpec((1, BAND_TQ, w), lambda bi, i: (bi, jnp.maximum(i - back, 0), col))

    return pl.pallas_call(
        _band_attn_body,
        grid=(b, s // BAND_TQ),
        in_specs=[pl.BlockSpec((1, BAND_TQ, w), lambda bi, i: (bi, i, cq)),
                  kv_spec(ck, 2), kv_spec(ck, 1), kv_spec(ck, 0),
                  kv_spec(cv, 2), kv_spec(cv, 1), kv_spec(cv, 0),
                  pl.BlockSpec((HB, BAND_TQ, BAND_TK), lambda bi, i: (0, 0, 0))],
        out_specs=pl.BlockSpec((1, BAND_TQ, w), lambda bi, i: (bi, i, 0)),
        out_shape=jax.ShapeDtypeStruct((b, s, w), BF16),
        compiler_params=_params(("parallel", "parallel")),
        name="band_attn",
    )(hm3, hm3, hm3, hm3, hm3, hm3, hm3, bias)


I16_MIN = -2 ** 15
NEG_INF_KEY = -1900671691
UNSELECTED = -1e33
DSA_T = 256


def _order_key(x):
    bits = lax.bitcast_convert_type(x, I32)
    return jnp.where(bits < 0, bits ^ jnp.int32(0x7FFFFFFF), bits)


def _dsa_body(q_ref, k_ref, v_ref, qi_ref, ki_ref, w_ref, o_ref, key_sc, half_sc, base_sc, z_sc, p_sc,
              *state, n_sel, slopes):
    t = DSA_T
    qi = pl.program_id(1)
    q0 = qi * t
    nblk = qi + 1
    kk = lax.broadcasted_iota(I32, (t, t), 0)
    tg = q0 + lax.broadcasted_iota(I32, (t, t), 1)

    def block_start(j):
        return pl.multiple_of(j * t, t)

    qidx = qi_ref[0]
    lane_i = lax.broadcasted_iota(I32, (1, HI * DI), 1)
    zero_i = jnp.zeros_like(qidx)
    qm = [jnp.where((lane_i >> 5) == h, qidx, zero_i) for h in range(HI)]
    wt = w_ref[0].T

    def score_step(j, carry):
        k0 = block_start(j)
        kib = ki_ref[0, pl.ds(k0, t), :]
        sc = jnp.zeros((t, t), F32)
        for h in range(HI):
            rel = jnp.maximum(lax.dot_general(kib, qm[h], _NT, preferred_element_type=F32), 0.0)
            sc = sc + wt[h:h + 1, :] * rel
        allowed = ((k0 + kk) >> CHUNK_SHIFT) <= (tg >> CHUNK_SHIFT)
        key = _order_key(jnp.where(allowed, sc, NEG_INF))
        key_sc[pl.ds(k0, t), :] = key
        half_sc[pl.ds(k0, t), :] = (key >> 16).astype(I16)
        return carry

    lax.fori_loop(0, nblk, score_step, 0)

    kf = float(n_sel)

    def count_half(cmp):
        def blk(j, acc):
            hit = jnp.where(cmp(half_sc[pl.ds(block_start(j), t), :]), jnp.int16(1), jnp.int16(0))
            for g in range(t // 32):
                acc = acc + hit[g * 32:(g + 1) * 32, :]
            return acc
        acc = lax.fori_loop(0, nblk, blk, jnp.zeros((32, t), I16))
        return jnp.sum(acc.astype(F32), axis=0, keepdims=True)

    def search_half(need):
        def count_ge(cand):
            c16 = cand.astype(I16)
            return count_half(lambda x: x >= c16)

        best = jnp.where(count_ge(jnp.zeros((1, t), I32)) >= need, 0, I16_MIN).astype(I32)

        def bit_step(i, best):
            cand = best + jnp.left_shift(jnp.int32(1), 14 - i)
            return jnp.where(count_ge(cand) >= need, cand, best)

        return lax.fori_loop(0, 15, bit_step, best)

    upper = search_half(kf)
    upper16 = upper.astype(I16)
    n_above = count_half(lambda x: x > upper16)

    def lower_step(j, carry):
        k0 = block_start(j)
        key = key_sc[pl.ds(k0, t), :]
        low = (key & 0xFFFF) + I16_MIN
        half_sc[pl.ds(k0, t), :] = jnp.where((key >> 16) == upper, low, I16_MIN).astype(I16)
        return carry

    lax.fori_loop(0, nblk, lower_step, 0)
    lower = search_half(kf - n_above)
    thr = upper * 65536 + (lower - I16_MIN)

    def alibi_or_unselected(keep, k0):
        kg = k0 + kk
        allowed = (kg >> CHUNK_SHIFT) <= (tg >> CHUNK_SHIFT)
        return jnp.where(keep, jnp.where(allowed, -jnp.abs(tg - kg).astype(F32), UNSELECTED), UNSELECTED)

    def select_step(j, n_ge):
        k0 = block_start(j)
        ge = key_sc[pl.ds(k0, t), :] >= thr
        base_sc[pl.ds(k0, t), :] = alibi_or_unselected(ge, k0)
        return n_ge + _col_sum32(jnp.where(ge, 1.0, 0.0))

    n_ge = jnp.sum(lax.fori_loop(0, nblk, select_step, jnp.zeros((32, t), F32)), axis=0, keepdims=True)
    over = jnp.where(n_ge > kf, jnp.where(thr != NEG_INF_KEY, 1.0, 0.0), 0.0)

    @pl.when(jnp.max(over) > 0.0)
    def _():
        def gt_step(j, acc):
            kb = key_sc[pl.ds(block_start(j), t), :]
            return acc + _col_sum32(jnp.where(kb > thr, 1.0, 0.0))
        n_gt = jnp.sum(lax.fori_loop(0, nblk, gt_step, jnp.zeros((32, t), F32)), axis=0, keepdims=True)
        need = kf - n_gt
        earlier = jnp.where(lax.broadcasted_iota(I32, (t, t), 1) < kk, 1.0, 0.0).astype(BF16)

        def tie_step(j, offs):
            k0 = block_start(j)
            kb = key_sc[pl.ds(k0, t), :]
            e = jnp.where(kb == thr, 1.0, 0.0)
            rank = jnp.dot(earlier, e.astype(BF16), preferred_element_type=F32) + offs
            keep = jnp.where(kb > thr, 1.0, jnp.where(rank < need, e, 0.0))
            base_sc[pl.ds(k0, t), :] = alibi_or_unselected(keep > 0.0, k0)
            return offs + jnp.sum(e, axis=0, keepdims=True)

        lax.fori_loop(0, nblk, tie_step, jnp.zeros((1, t), F32))

    q = q_ref[0]
    lane = lax.broadcasted_iota(I32, (1, HC * DC), 1)
    zero = jnp.zeros_like(q)
    q_head = [jnp.where((lane >> 6) == h, q, zero) for h in range(HC)]
    m_sc, l_sc, acc_sc = state[:HC], state[HC:2 * HC], state[2 * HC:]
    for h in range(HC):
        m_sc[h][...] = jnp.full((1, t), NEG_INF, F32)
        l_sc[h][...] = jnp.zeros((1, t), F32)
        acc_sc[h][...] = jnp.zeros((DC, t), F32)

    def att_step(j, carry):
        k0 = block_start(j)
        kb = k_ref[0, pl.ds(k0, t), :]
        base = base_sc[pl.ds(k0, t), :]
        zmax = []
        for h in range(HC):
            z = lax.dot_general(kb, q_head[h], _NT, preferred_element_type=F32) + slopes[h] * base
            z_sc[h] = z
            zmax.append(jnp.max(z, axis=0, keepdims=True))
        alphas = []
        for h in range(HC):
            m_old = m_sc[h][...]
            m_new = jnp.maximum(m_old, zmax[h])
            alpha = jnp.exp(m_old - m_new)
            p = jnp.exp(z_sc[h] - m_new)
            l_sc[h][...] = alpha * l_sc[h][...] + jnp.sum(p, axis=0, keepdims=True)
            m_sc[h][...] = m_new
            p_sc[h] = p.astype(BF16)
            alphas.append(alpha)
        vt = v_ref[0, pl.ds(k0, t), :].T
        for h in range(HC):
            acc_sc[h][...] = alphas[h] * acc_sc[h][...] + jnp.dot(
                vt[h * DC:(h + 1) * DC, :], p_sc[h], preferred_element_type=F32)
        return carry

    lax.fori_loop(0, nblk, att_step, 0)
    heads = [acc_sc[h][...] / l_sc[h][...] for h in range(HC)]
    o_ref[0] = jnp.concatenate(heads, axis=0).T.astype(BF16)


def _dsa_attn(hm3, hik3, hw3, slopes):
    b, s, _ = hm3.shape
    t = DSA_T
    w = HC * DC
    cq, ck, cv = 9, 10, 11
    body = functools.partial(_dsa_body, n_sel=min(TOPK_MAX, s // 4), slopes=slopes)
    return pl.pallas_call(
        body,
        grid=(b, s // t),
        in_specs=[pl.BlockSpec((1, t, w), lambda bi, i: (bi, i, cq)),
                  pl.BlockSpec((1, s, w), lambda bi, i: (bi, 0, ck)),
                  pl.BlockSpec((1, s, w), lambda bi, i: (bi, 0, cv)),
                  pl.BlockSpec((1, t, HI * DI), lambda bi, i: (bi, i, 0)),
                  pl.BlockSpec((1, s, HI * DI), lambda bi, i: (bi, 0, 1)),
                  pl.BlockSpec((1, t, D_W), lambda bi, i: (bi, i, 0))],
        out_specs=pl.BlockSpec((1, t, w), lambda bi, i: (bi, i, 0)),
        out_shape=jax.ShapeDtypeStruct((b, s, w), BF16),
        scratch_shapes=([pltpu.VMEM((s, t), I32), pltpu.VMEM((s, t), I16), pltpu.VMEM((s, t), F32),
                         pltpu.VMEM((HC, t, t), F32), pltpu.VMEM((HC, t, t), BF16)]
                        + [pltpu.VMEM((1, t), F32)] * (2 * HC) + [pltpu.VMEM((DC, t), F32)] * HC),
        compiler_params=_params(("parallel", "arbitrary")),
        name="dsa_attn",
    )(hm3, hm3, hm3, hik3, hik3, hw3)


def _layer_norm(y, g, b):
    mu = jnp.mean(y, axis=-1, keepdims=True)
    d = y - mu
    var = jnp.mean(d * d, axis=-1, keepdims=True)
    return d * lax.rsqrt(var + EPS) * g + b


def _out_router_body(a_ref, b_ref, c_ref, x_ref, wo_ref, g_ref, beta_ref, wr_ref, br_ref,
                     x1_ref, meta_ref, cnt_ref, *, tm):
    i = pl.program_id(0)
    wa = HA * 2 * DA
    wb = wa + HB * DB
    y = (jnp.dot(a_ref[...], wo_ref[0:wa, :], preferred_element_type=F32)
         + jnp.dot(b_ref[...], wo_ref[wa:wb, :], preferred_element_type=F32)
         + jnp.dot(c_ref[...], wo_ref[wb:, :], preferred_element_type=F32))
    x1 = _layer_norm(DEEPNORM_ALPHA * x_ref[...] + y, g_ref[...], beta_ref[...])
    x1_ref[...] = x1

    logits = lax.dot_general(wr_ref[...], x1, _NT, preferred_element_type=F32,
                             precision=lax.Precision.HIGHEST) + br_ref[...]
    row_e = lax.broadcasted_iota(I32, (N_EXPERTS, tm), 0).astype(F32)
    sel = jnp.zeros((N_EXPERTS, tm), F32)
    work = logits
    picks, vals = [], []
    for _ in range(TOP_K):
        v = jnp.max(work, axis=0, keepdims=True)
        e = jnp.min(jnp.where(work == v, row_e, float(N_EXPERTS)), axis=0, keepdims=True)
        hit = row_e == e
        sel = jnp.where(hit, 1.0, sel)
        work = jnp.where(hit, -jnp.inf, work)
        picks.append(e)
        vals.append(v)
    ex = [jnp.exp(v - vals[0]) for v in vals]
    den = ex[0] + ex[1] + ex[2] + ex[3]

    @pl.when(i == 0)
    def _():
        cnt_ref[...] = jnp.zeros_like(cnt_ref)

    earlier = jnp.where(lax.broadcasted_iota(I32, (tm, tm), 0) < lax.broadcasted_iota(I32, (tm, tm), 1),
                        1.0, 0.0).astype(BF16)
    so_far = cnt_ref[...]
    before = jnp.dot(sel.astype(BF16), earlier, preferred_element_type=F32) + so_far[:, 0:1]
    cnt_ref[...] = so_far + jnp.sum(sel, axis=1, keepdims=True)

    rows = lax.broadcasted_iota(I32, (4 * TOP_K, tm), 0)
    meta_t = jnp.zeros((4 * TOP_K, tm), F32)
    for kk in range(TOP_K):
        rank = jnp.sum(jnp.where(row_e == picks[kk], before, 0.0), axis=0, keepdims=True)
        meta_t = jnp.where(rows == kk, picks[kk], meta_t)
        meta_t = jnp.where(rows == TOP_K + kk, rank, meta_t)
        meta_t = jnp.where(rows == 2 * TOP_K + kk, ex[kk] / den, meta_t)
    meta_ref[...] = jnp.concatenate([meta_t, jnp.zeros((128 - 4 * TOP_K, tm), F32)], axis=0).T


def _out_router(oa, ob, oc, x2d, wo, g, beta, wr, br):
    t = x2d.shape[0]
    tm = 512
    row = lambda i: (i, 0)
    fix = lambda i: (0, 0)
    body = functools.partial(_out_router_body, tm=tm)
    return pl.pallas_call(
        body,
        grid=(t // tm,),
        in_specs=[pl.BlockSpec((tm, oa.shape[1]), row), pl.BlockSpec((tm, ob.shape[1]), row),
                  pl.BlockSpec((tm, oc.shape[1]), row), pl.BlockSpec((tm, D_MODEL), row),
                  pl.BlockSpec((D_MIX, D_MODEL), fix), pl.BlockSpec((1, D_MODEL), fix),
                  pl.BlockSpec((1, D_MODEL), fix), pl.BlockSpec((N_EXPERTS, D_MODEL), fix),
                  pl.BlockSpec((N_EXPERTS, 1), fix)],
        out_specs=[pl.BlockSpec((tm, D_MODEL), row),
                   pl.BlockSpec((tm, 128), row), pl.BlockSpec((N_EXPERTS, 128), fix)],
        out_shape=[jax.ShapeDtypeStruct((t, D_MODEL), F32),
                   jax.ShapeDtypeStruct((t, 128), F32), jax.ShapeDtypeStruct((N_EXPERTS, 128), F32)],
        compiler_params=_params(("arbitrary",)),
        name="out_router",
    )(oa, ob, oc, x2d, wo, g, beta, wr, br)


MOE_ROWS = 512
DISPATCH_TOKENS = 256


def _dispatch_body(dest_ref, pad_ref, x_ref, rows_out, zero_sc, sem, zero_sem):
    @pl.when(pl.program_id(0) == 0)
    def _():
        zero_sc[...] = jnp.zeros_like(zero_sc)
        n_blocks = pad_ref.shape[0]

        def zero_copy(b):
            start = pl.multiple_of(b * MOE_ROWS, MOE_ROWS)
            return pltpu.make_async_copy(zero_sc, rows_out.at[pl.ds(start, MOE_ROWS), :], zero_sem)

        def zero_start(b, carry):
            @pl.when(pad_ref[b] > 0)
            def _():
                zero_copy(b).start()
            return carry

        def zero_wait(b, carry):
            @pl.when(pad_ref[b] > 0)
            def _():
                zero_copy(b).wait()
            return carry

        lax.fori_loop(0, n_blocks, zero_start, 0)
        lax.fori_loop(0, n_blocks, zero_wait, 0)

    def token_pair(r2, carry):
        for u in range(2):
            r = 2 * r2 + u
            for kk in range(TOP_K):
                d = dest_ref[r * TOP_K + kk]
                pltpu.make_async_copy(x_ref.at[pl.ds(r, 1), :], rows_out.at[pl.ds(d, 1), :],
                                      sem).start(priority=kk % 2)
        return carry

    lax.fori_loop(0, DISPATCH_TOKENS // 2, token_pair, 0)
    for _ in range(TOP_K):
        pltpu.make_async_copy(x_ref, x_ref, sem).wait()


def _dispatch(dest_flat, block_padded, x1, n_rows):
    t = x1.shape[0]
    tm = DISPATCH_TOKENS
    return pl.pallas_call(
        _dispatch_body,
        grid=(t // tm,),
        in_specs=[pl.BlockSpec((tm * TOP_K,), lambda i: (i,), memory_space=pltpu.SMEM),
                  pl.BlockSpec(memory_space=pltpu.SMEM),
                  pl.BlockSpec((tm, D_MODEL), lambda i: (i, 0))],
        out_specs=pl.BlockSpec(memory_space=pl.ANY),
        out_shape=jax.ShapeDtypeStruct((n_rows, D_MODEL), F32),
        scratch_shapes=[pltpu.VMEM((MOE_ROWS, D_MODEL), F32), pltpu.SemaphoreType.DMA(()),
                        pltpu.SemaphoreType.DMA(())],
        compiler_params=_params(("arbitrary",)),
        name="moe_dispatch",
    )(dest_flat, block_padded, x1)


def _moe_body(be_ref, bv_ref, src_ref, x_ref, wgu_ref, bgu_ref, wd_ref, bd_ref, y_ref, wgu_sc, wd_sc):
    del src_ref
    i = pl.program_id(0)
    prev = be_ref[jnp.maximum(i - 1, 0)]

    @pl.when((i == 0) | (be_ref[i] != prev))
    def _():
        wgu_sc[...] = wgu_ref[0, 0].astype(BF16)
        wd_sc[...] = wd_ref[0, 0].astype(BF16)

    @pl.when(bv_ref[i] > 0)
    def _():
        h = jnp.dot(x_ref[...].astype(BF16), wgu_sc[...], preferred_element_type=F32) + bgu_ref[0, 0]
        gate = jnp.minimum(h[:, :D_EXPERT], SWIGLU_LIMIT)
        up = jnp.clip(h[:, D_EXPERT:], -SWIGLU_LIMIT, SWIGLU_LIMIT)
        glu = gate * (1.0 / (1.0 + jnp.exp(-SWIGLU_ALPHA * gate)))
        act = ((up + 1.0) * glu).astype(BF16)
        y_ref[...] = jnp.dot(act, wd_sc[...], preferred_element_type=F32) + bd_ref[0, 0]

    @pl.when(bv_ref[i] == 0)
    def _():
        y_ref[...] = jnp.zeros_like(y_ref)


def _moe(layer, block_expert, block_valid, block_src, x_rows, wgu, bgu, wd, bd):
    n_rows = x_rows.shape[0]
    expert = lambda i, be, bv, src: (layer, be[i], 0, 0)
    grid_spec = pltpu.PrefetchScalarGridSpec(
        num_scalar_prefetch=3,
        grid=(n_rows // MOE_ROWS,),
        in_specs=[pl.BlockSpec((MOE_ROWS, D_MODEL), lambda i, be, bv, src: (src[i], 0)),
                  pl.BlockSpec((1, 1, D_MODEL, 2 * D_EXPERT), expert),
                  pl.BlockSpec((1, 1, 1, 2 * D_EXPERT), expert),
                  pl.BlockSpec((1, 1, D_EXPERT, D_MODEL), expert),
                  pl.BlockSpec((1, 1, 1, D_MODEL), expert)],
        out_specs=pl.BlockSpec((MOE_ROWS, D_MODEL), lambda i, be, bv, src: (i, 0)),
        scratch_shapes=[pltpu.VMEM((D_MODEL, 2 * D_EXPERT), BF16), pltpu.VMEM((D_EXPERT, D_MODEL), BF16)],
    )
    return pl.pallas_call(
        _moe_body,
        grid_spec=grid_spec,
        out_shape=jax.ShapeDtypeStruct((n_rows, D_MODEL), F32),
        compiler_params=_params(("arbitrary",)),
        name="moe_experts",
    )(block_expert, block_valid, block_src, x_rows, wgu, bgu, wd, bd)


def _combine_body(x_ref, y0_ref, y1_ref, y2_ref, y3_ref, meta_ref, g_ref, beta_ref, o_ref):
    gates = meta_ref[...]
    ffn = (gates[:, 8:9] * y0_ref[...] + gates[:, 9:10] * y1_ref[...]
           + gates[:, 10:11] * y2_ref[...] + gates[:, 11:12] * y3_ref[...])
    o_ref[...] = _layer_norm(DEEPNORM_ALPHA * x_ref[...] + ffn, g_ref[...], beta_ref[...])


def _combine(x1, ys, meta, g, beta):
    t = x1.shape[0]
    tm = 512
    row = lambda i: (i, 0)
    fix = lambda i: (0, 0)
    return pl.pallas_call(
        _combine_body,
        grid=(t // tm,),
        in_specs=[pl.BlockSpec((tm, D_MODEL), row)] * 5
                 + [pl.BlockSpec((tm, 128), row), pl.BlockSpec((1, D_MODEL), fix),
                    pl.BlockSpec((1, D_MODEL), fix)],
        out_specs=pl.BlockSpec((tm, D_MODEL), row),
        out_shape=jax.ShapeDtypeStruct((t, D_MODEL), F32),
        compiler_params=_params(("parallel",)),
        name="combine_ln",
    )(x1, *ys, meta, g, beta)


def _prep_w_in(w):
    scale = DA ** -0.5
    col = jnp.ones((D_QKV,), F32)
    for lo, width in ((0, HA * 2 * DA), (3 * HA * 2 * DA, HB * DB),
                      (3 * HA * 2 * DA + 3 * HB * DB, HC * DC)):
        col = col.at[lo:lo + width].set(scale)
    main = w[:, :D_QKV] * col
    q_idx = w[:, D_QKV:D_QKV + HI * DI]
    k_idx = w[:, D_QKV + HI * DI:D_QKV + HI * DI + DI]
    w_idx = w[:, D_QKV + HI * DI + DI:] * ((HI ** -0.5) * (DI ** -0.5))
    w_pad = jnp.zeros((D_MODEL, D_W - HI), F32)
    return jnp.concatenate([main, q_idx, jnp.tile(k_idx, (1, HI)), w_idx, w_pad], axis=1).astype(BF16)


def _route(meta, cnt, n_tok):
    top_e = meta[:, 0:TOP_K].astype(I32)
    rank = meta[:, TOP_K:2 * TOP_K].astype(I32)
    counts = cnt[:, 0].astype(I32)
    n_blocks = n_tok * TOP_K // MOE_ROWS + N_EXPERTS
    blocks_e = (counts + MOE_ROWS - 1) // MOE_ROWS
    blocks_end = jnp.cumsum(blocks_e)
    row_start = (blocks_end - blocks_e) * MOE_ROWS
    experts = jnp.arange(N_EXPERTS, dtype=I32)
    start_of = jnp.sum(jnp.where(top_e[:, :, None] == experts, row_start, 0), axis=-1)
    dest = start_of + rank
    blk = jnp.arange(n_blocks, dtype=I32)
    block_expert = jnp.minimum(jnp.sum((blocks_end[None, :] <= blk[:, None]).astype(I32), axis=1),
                               N_EXPERTS - 1)
    block_valid = (blk < blocks_end[-1]).astype(I32)
    block_src = jnp.minimum(blk, blocks_end[-1] - 1)
    is_last = jnp.any((blocks_end[None, :] == blk[:, None] + 1) & (blocks_e[None, :] > 0), axis=1)
    block_padded = jnp.where(block_valid > 0, is_last, True).astype(I32)
    return dest, block_padded, n_blocks * MOE_ROWS, block_expert, block_valid, block_src


def kernel(x, w_in, lam_q1, lam_k1, lam_q2, lam_k2, subln_g, rel_bias, w_out, ln1_g, ln1_b,
           w_router, b_router, w_gu, b_gu, w_down, b_down, ln2_g, ln2_b):
    bsz, seq, _ = x.shape
    n_tok = bsz * seq
    slopes = [2.0 ** (-8.0 * i / (HA + HC)) for i in range(1, HA + HC + 1)]
    slopes_a = tuple(slopes[0::2])
    slopes_c = tuple(slopes[1::2])
    x2d = x.reshape(n_tok, D_MODEL)
    for l in range(DEPTH):
        lam_init = 0.8 - 0.6 * math.exp(-0.3 * l)
        lam = (jnp.exp(jnp.sum(lam_q1[l] * lam_k1[l])) - jnp.exp(jnp.sum(lam_q2[l] * lam_k2[l]))
               + lam_init).reshape(1).astype(F32)
        hm, hik, hw = _in_proj(x2d, _prep_w_in(w_in[l]))
        hm3 = hm.reshape(bsz, seq, D_QKV)
        out_a = _diff_attn(hm3, slopes_a, lam, subln_g[l].reshape(1, 2 * DA), lam_init)
        out_b = _band_attn(hm3, _band_bias(rel_bias[l]))
        out_c = _dsa_attn(hm3, hik.reshape(bsz, seq, D_IK), hw.reshape(bsz, seq, D_W), slopes_c)
        x1, meta, cnt = _out_router(
            out_a.reshape(n_tok, -1), out_b.reshape(n_tok, -1), out_c.reshape(n_tok, -1), x2d,
            w_out[l].astype(BF16), ln1_g[l].reshape(1, -1), ln1_b[l].reshape(1, -1),
            w_router[l].T, b_router[l].reshape(-1, 1))
        dest, block_padded, n_rows, block_expert, block_valid, block_src = _route(meta, cnt, n_tok)
        x_rows = _dispatch(dest.reshape(-1), block_padded, x1, n_rows)
        y_rows = _moe(l, block_expert, block_valid, block_src, x_rows, w_gu,
                      b_gu.reshape(DEPTH, N_EXPERTS, 1, -1), w_down, b_down.reshape(DEPTH, N_EXPERTS, 1, -1))
        ys = [y_rows[dest[:, kk]] for kk in range(TOP_K)]
        x2d = _combine(x1, ys, meta, ln2_g[l].reshape(1, -1), ln2_b[l].reshape(1, -1))
    return x2d.reshape(bsz, seq, D_MODEL)
```

```python
import functools
import math

import jax
import jax.numpy as jnp
from jax import lax
from jax.experimental import pallas as pl
from jax.experimental.pallas import tpu as pltpu

F32 = jnp.float32
BF16 = jnp.bfloat16
I32 = jnp.int32
I16 = jnp.int16

D_MODEL = 1024
DEPTH = 4
CHUNK = 64
CHUNK_SHIFT = 6
HA, DA = 4, 64
HB, DB = 4, 64
BAND_CHUNKS = 8
REL_CLIP = 128
HC, DC = 4, 64
HI, DI = 8, 32
TOPK_MAX = 256
N_EXPERTS = 32
TOP_K = 4
D_EXPERT = D_MODEL
SWIGLU_LIMIT = 7.0
SWIGLU_ALPHA = 1.702
DEEPNORM_ALPHA = (2 * DEPTH) ** 0.25
EPS = 1e-5
NEG_INF = -1e30

D_QKV = 3 * HA * 2 * DA + 3 * HB * DB + 3 * HC * DC
D_IK = 2 * HI * DI
D_W = 128
D_MIX = HA * 2 * DA + HB * DB + HC * DC

VMEM_LIMIT = 56 * 1024 * 1024

_NT = (((1,), (1,)), ((), ()))
_TN = (((0,), (0,)), ((), ()))


def _params(sem):
    return pltpu.CompilerParams(dimension_semantics=sem, vmem_limit_bytes=VMEM_LIMIT)


def _col_sum32(x):
    n, t = x.shape
    return x.reshape(n // 32, 32, t).sum(axis=0)


def _in_proj_body(x_ref, w_ref, hm_ref, hik_ref, hw_ref):
    xb = x_ref[...].astype(BF16)
    for n0 in range(0, D_QKV, 512):
        hm_ref[:, n0:n0 + 512] = jnp.dot(
            xb, w_ref[:, n0:n0 + 512], preferred_element_type=F32).astype(BF16)
    hik_ref[...] = jnp.dot(xb, w_ref[:, D_QKV:D_QKV + D_IK], preferred_element_type=F32).astype(BF16)
    hw_ref[...] = jnp.dot(xb, w_ref[:, D_QKV + D_IK:], preferred_element_type=F32)


def _in_proj(x2d, w):
    t = x2d.shape[0]
    tm = 512
    dw = D_QKV + D_IK + D_W
    return pl.pallas_call(
        _in_proj_body,
        grid=(t // tm,),
        in_specs=[pl.BlockSpec((tm, D_MODEL), lambda i: (i, 0)),
                  pl.BlockSpec((D_MODEL, dw), lambda i: (0, 0))],
        out_specs=[pl.BlockSpec((tm, D_QKV), lambda i: (i, 0)),
                   pl.BlockSpec((tm, D_IK), lambda i: (i, 0)),
                   pl.BlockSpec((tm, D_W), lambda i: (i, 0))],
        out_shape=[jax.ShapeDtypeStruct((t, D_QKV), BF16),
                   jax.ShapeDtypeStruct((t, D_IK), BF16),
                   jax.ShapeDtypeStruct((t, D_W), F32)],
        compiler_params=_params(("parallel",)),
        name="in_proj",
    )(x2d, w)


def _diff_attn_body(lam_ref, q_ref, k_ref, v_ref, g_ref, o_ref,
                    bias_sc, z_sc, z2_sc, p_sc, *state, t, slopes, out_scale):
    qi = pl.program_id(1)
    lam = lam_ref[0]
    dh = 2 * DA
    lane = lax.broadcasted_iota(I32, (1, dh), 1)
    q_half = []
    for h in range(HA):
        qh = q_ref[0, :, h * dh:(h + 1) * dh]
        zero = jnp.zeros_like(qh)
        q_half.append((jnp.where(lane < DA, qh, zero), jnp.where(lane >= DA, qh, zero)))

    @pl.when(qi == 0)
    def _():
        kk = lax.broadcasted_iota(I32, (t, t), 0)
        tt = lax.broadcasted_iota(I32, (t, t), 1)
        own_chunk = (kk >> CHUNK_SHIFT) <= (tt >> CHUNK_SHIFT)
        for h in range(HA):
            bias_sc[h] = slopes[h] * kk.astype(F32)
            bias_sc[HA + h] = jnp.where(
                own_chunk, slopes[h] * jnp.minimum(kk, 2 * tt - kk).astype(F32), NEG_INF)

    n_chain = 2 * HA
    m_sc, l_sc, acc_sc = state[:n_chain], state[n_chain:2 * n_chain], state[2 * n_chain:]
    for c in range(n_chain):
        m_sc[c][...] = jnp.full((1, t), NEG_INF, F32)
        l_sc[c][...] = jnp.zeros((1, t), F32)
        acc_sc[c][...] = jnp.zeros((dh, t), F32)

    def scores(j, z_dst):
        jb = jnp.minimum(j, qi)
        k0 = pl.multiple_of(jb * t, t)
        bias_base = jnp.where(jb == qi, HA, 0)
        for h in range(HA):
            kb = k_ref[0, pl.ds(k0, t), h * dh:(h + 1) * dh]
            bias = bias_sc[bias_base + h]
            for a in range(2):
                z_dst[2 * h + a] = lax.dot_general(
                    kb, q_half[h][a], _NT, preferred_element_type=F32) + bias

    def accumulate(j, z_src):
        k0 = pl.multiple_of(j * t, t)
        k0f = k0.astype(F32)
        alphas = []
        for c in range(n_chain):
            shift = slopes[c // 2] * k0f
            z = z_src[c]
            m_old = m_sc[c][...]
            m_new = jnp.maximum(m_old, jnp.max(z, axis=0, keepdims=True) + shift)
            alpha = jnp.exp(m_old - m_new)
            p = jnp.exp(z - (m_new - shift))
            l_sc[c][...] = alpha * l_sc[c][...] + jnp.sum(p, axis=0, keepdims=True)
            m_sc[c][...] = m_new
            p_sc[c] = p.astype(BF16)
            alphas.append(alpha)
        for h in range(HA):
            vt = v_ref[0, pl.ds(k0, t), h * dh:(h + 1) * dh].T
            for a in range(2):
                c = 2 * h + a
                acc_sc[c][...] = alphas[c] * acc_sc[c][...] + jnp.dot(
                    vt, p_sc[c], preferred_element_type=F32)

    scores(0, z_sc)

    def pair_step(i, carry):
        scores(2 * i + 1, z2_sc)
        accumulate(2 * i, z_sc)

        @pl.when(2 * i + 1 <= qi)
        def _():
            scores(2 * i + 2, z_sc)
            accumulate(2 * i + 1, z2_sc)
        return carry

    lax.fori_loop(0, (qi + 2) // 2, pair_step, 0)

    outs = []
    for h in range(HA):
        ot = (acc_sc[2 * h][...] / l_sc[2 * h][...]
              - lam * (acc_sc[2 * h + 1][...] / l_sc[2 * h + 1][...]))
        ms = jnp.mean(ot * ot, axis=0, keepdims=True)
        outs.append((ot * lax.rsqrt(ms + EPS)).T * g_ref[...] * out_scale)
    o_ref[0] = jnp.concatenate(outs, axis=1).astype(BF16)


def _diff_attn(hm3, slopes, lam, sub_g, lam_init):
    b, s, _ = hm3.shape
    t = 256
    dh = 2 * DA
    w = HA * dh
    body = functools.partial(_diff_attn_body, t=t, slopes=slopes, out_scale=1.0 - lam_init)
    return pl.pallas_call(
        body,
        grid=(b, s // t),
        in_specs=[pl.BlockSpec(memory_space=pltpu.SMEM),
                  pl.BlockSpec((1, t, w), lambda bi, qi: (bi, qi, 0)),
                  pl.BlockSpec((1, s, w), lambda bi, qi: (bi, 0, 1)),
                  pl.BlockSpec((1, s, w), lambda bi, qi: (bi, 0, 2)),
                  pl.BlockSpec((1, dh), lambda bi, qi: (0, 0))],
        out_specs=pl.BlockSpec((1, t, w), lambda bi, qi: (bi, qi, 0)),
        out_shape=jax.ShapeDtypeStruct((b, s, w), BF16),
        scratch_shapes=([pltpu.VMEM((2 * HA, t, t), F32), pltpu.VMEM((2 * HA, t, t), F32),
                         pltpu.VMEM((2 * HA, t, t), F32), pltpu.VMEM((2 * HA, t, t), BF16)]
                        + [pltpu.VMEM((1, t), F32)] * (4 * HA) + [pltpu.VMEM((dh, t), F32)] * (2 * HA)),
        compiler_params=_params(("parallel", "arbitrary")),
        name="diff_attn",
    )(lam, hm3, hm3, hm3, sub_g)


BAND_TQ = 256
BAND_TK = 3 * BAND_TQ


def _band_attn_body(q_ref, k0_ref, k1_ref, k2_ref, v0_ref, v1_ref, v2_ref, bias_ref, o_ref):
    i = pl.program_id(1)
    q = q_ref[0]
    k = jnp.concatenate([k0_ref[0], k1_ref[0], k2_ref[0]], axis=0)
    v = jnp.concatenate([v0_ref[0], v1_ref[0], v2_ref[0]], axis=0)
    col = lax.broadcasted_iota(I32, (1, BAND_TK), 1)
    first_valid = jnp.where(i >= 2, 0, jnp.where(i >= 1, BAND_TQ, 2 * BAND_TQ))
    kvalid = col >= first_valid
    lane = lax.broadcasted_iota(I32, (1, HB * DB), 1)
    zero = jnp.zeros_like(q)
    out = jnp.zeros((BAND_TQ, HB * DB), F32)
    for h in range(HB):
        head = (lane >> 6) == h
        s = lax.dot_general(jnp.where(head, q, zero), k, _NT, preferred_element_type=F32)
        s = jnp.where(kvalid, s + bias_ref[h], NEG_INF)
        m = jnp.max(s, axis=-1, keepdims=True)
        p = jnp.exp(s - m)
        l = jnp.sum(p, axis=-1, keepdims=True)
        o = jnp.dot(p.astype(BF16), v, preferred_element_type=F32) / l
        out = jnp.where(head, o, out)
    o_ref[0] = out.astype(BF16)


def _band_bias(rel_bias):
    r = jnp.arange(BAND_TQ)[:, None]
    j = jnp.arange(BAND_TK)[None, :]
    qc = r // CHUNK
    kc = j // CHUNK
    in_band = (kc >= qc) & (kc <= qc + BAND_CHUNKS)
    n = BAND_TQ + BAND_TK
    m = jnp.arange(n)
    delta = jnp.where(m < BAND_TK, m, m - n)
    by_delta = rel_bias.astype(F32)[:, jnp.clip(2 * BAND_TQ - delta, -REL_CLIP, REL_CLIP) + REL_CLIP]
    flat = jnp.tile(by_delta, (1, BAND_TQ))[:, :BAND_TQ * (n - 1)]
    bias = flat.reshape(HB, BAND_TQ, n - 1)[:, :, :BAND_TK]
    return jnp.where(in_band[None], bias, NEG_INF)


def _band_attn(hm3, bias):
    b, s, _ = hm3.shape
    w = HB * DB
    cq, ck, cv = 6, 7, 8

    def kv_spec(col, back):
        return pl.BlockSpec((1, BAND_TQ, w), lambda bi, i: (bi, jnp.maximum(i - back, 0), col))

    return pl.pallas_call(
        _band_attn_body,
        grid=(b, s // BAND_TQ),
        in_specs=[pl.BlockSpec((1, BAND_TQ, w), lambda bi, i: (bi, i, cq)),
                  kv_spec(ck, 2), kv_spec(ck, 1), kv_spec(ck, 0),
                  kv_spec(cv, 2), kv_spec(cv, 1), kv_spec(cv, 0),
                  pl.BlockSpec((HB, BAND_TQ, BAND_TK), lambda bi, i: (0, 0, 0))],
        out_specs=pl.BlockSpec((1, BAND_TQ, w), lambda bi, i: (bi, i, 0)),
        out_shape=jax.ShapeDtypeStruct((b, s, w), BF16),
        compiler_params=_params(("parallel", "parallel")),
        name="band_attn",
    )(hm3, hm3, hm3, hm3, hm3, hm3, hm3, bias)


I16_MIN = -2 ** 15
NEG_INF_KEY = -1900671691
UNSELECTED = -1e33
DSA_T = 256


def _order_key(x):
    bits = lax.bitcast_convert_type(x, I32)
    return jnp.where(bits < 0, bits ^ jnp.int32(0x7FFFFFFF), bits)


def _dsa_body(q_ref, k_ref, v_ref, qi_ref, ki_ref, w_ref, o_ref, key_sc, half_sc, base_sc, z_sc, z2_sc,
              p_sc, *state, n_sel, slopes):
    t = DSA_T
    qi = pl.program_id(1)
    q0 = qi * t
    nblk = qi + 1
    kk = lax.broadcasted_iota(I32, (t, t), 0)
    tg = q0 + lax.broadcasted_iota(I32, (t, t), 1)

    def block_start(j):
        return pl.multiple_of(j * t, t)

    qidx = qi_ref[0]
    lane_i = lax.broadcasted_iota(I32, (1, HI * DI), 1)
    zero_i = jnp.zeros_like(qidx)
    qm = [jnp.where((lane_i >> 5) == h, qidx, zero_i) for h in range(HI)]
    wt = w_ref[0].T

    def score_step(j, carry):
        k0 = block_start(j)
        kib = ki_ref[0, pl.ds(k0, t), :]
        sc = jnp.zeros((t, t), F32)
        for h in range(HI):
            rel = jnp.maximum(lax.dot_general(kib, qm[h], _NT, preferred_element_type=F32), 0.0)
            sc = sc + wt[h:h + 1, :] * rel
        allowed = ((k0 + kk) >> CHUNK_SHIFT) <= (tg >> CHUNK_SHIFT)
        key = _order_key(jnp.where(allowed, sc, NEG_INF))
        key_sc[pl.ds(k0, t), :] = key
        half_sc[pl.ds(k0, t), :] = (key >> 16).astype(I16)
        return carry

    lax.fori_loop(0, nblk, score_step, 0)

    kf = float(n_sel)

    def count_half(cmp):
        def blk(j, acc):
            hit = jnp.where(cmp(half_sc[pl.ds(block_start(j), t), :]), jnp.int16(1), jnp.int16(0))
            for g in range(t // 32):
                acc = acc + hit[g * 32:(g + 1) * 32, :]
            return acc
        acc = lax.fori_loop(0, nblk, blk, jnp.zeros((32, t), I16))
        return jnp.sum(acc.astype(F32), axis=0, keepdims=True)

    def search_half(need):
        def count_ge(cand):
            c16 = cand.astype(I16)
            return count_half(lambda x: x >= c16)

        best = jnp.where(count_ge(jnp.zeros((1, t), I32)) >= need, 0, I16_MIN).astype(I32)

        def bit_step(i, best):
            cand = best + jnp.left_shift(jnp.int32(1), 14 - i)
            return jnp.where(count_ge(cand) >= need, cand, best)

        return lax.fori_loop(0, 15, bit_step, best)

    upper = search_half(kf)
    upper16 = upper.astype(I16)
    n_above = count_half(lambda x: x > upper16)

    def lower_step(j, carry):
        k0 = block_start(j)
        key = key_sc[pl.ds(k0, t), :]
        low = (key & 0xFFFF) + I16_MIN
        half_sc[pl.ds(k0, t), :] = jnp.where((key >> 16) == upper, low, I16_MIN).astype(I16)
        return carry

    lax.fori_loop(0, nblk, lower_step, 0)
    lower = search_half(kf - n_above)
    thr = upper * 65536 + (lower - I16_MIN)

    def alibi_or_unselected(keep, k0):
        kg = k0 + kk
        allowed = (kg >> CHUNK_SHIFT) <= (tg >> CHUNK_SHIFT)
        return jnp.where(keep, jnp.where(allowed, -jnp.abs(tg - kg).astype(F32), UNSELECTED), UNSELECTED)

    def select_step(j, n_ge):
        k0 = block_start(j)
        ge = key_sc[pl.ds(k0, t), :] >= thr
        base_sc[pl.ds(k0, t), :] = alibi_or_unselected(ge, k0)
        return n_ge + _col_sum32(jnp.where(ge, 1.0, 0.0))

    n_ge = jnp.sum(lax.fori_loop(0, nblk, select_step, jnp.zeros((32, t), F32)), axis=0, keepdims=True)
    over = jnp.where(n_ge > kf, jnp.where(thr != NEG_INF_KEY, 1.0, 0.0), 0.0)

    @pl.when(jnp.max(over) > 0.0)
    def _():
        def gt_step(j, acc):
            kb = key_sc[pl.ds(block_start(j), t), :]
            return acc + _col_sum32(jnp.where(kb > thr, 1.0, 0.0))
        n_gt = jnp.sum(lax.fori_loop(0, nblk, gt_step, jnp.zeros((32, t), F32)), axis=0, keepdims=True)
        need = kf - n_gt
        earlier = jnp.where(lax.broadcasted_iota(I32, (t, t), 1) < kk, 1.0, 0.0).astype(BF16)

        def tie_step(j, offs):
            k0 = block_start(j)
            kb = key_sc[pl.ds(k0, t), :]
            e = jnp.where(kb == thr, 1.0, 0.0)
            rank = jnp.dot(earlier, e.astype(BF16), preferred_element_type=F32) + offs
            keep = jnp.where(kb > thr, 1.0, jnp.where(rank < need, e, 0.0))
            base_sc[pl.ds(k0, t), :] = alibi_or_unselected(keep > 0.0, k0)
            return offs + jnp.sum(e, axis=0, keepdims=True)

        lax.fori_loop(0, nblk, tie_step, jnp.zeros((1, t), F32))

    q = q_ref[0]
    lane = lax.broadcasted_iota(I32, (1, HC * DC), 1)
    zero = jnp.zeros_like(q)
    q_head = [jnp.where((lane >> 6) == h, q, zero) for h in range(HC)]
    m_sc, l_sc, acc_sc = state[:HC], state[HC:2 * HC], state[2 * HC:]
    for h in range(HC):
        m_sc[h][...] = jnp.full((1, t), NEG_INF, F32)
        l_sc[h][...] = jnp.zeros((1, t), F32)
        acc_sc[h][...] = jnp.zeros((DC, t), F32)

    def scores(j, z_dst):
        k0 = block_start(jnp.minimum(j, qi))
        kb = k_ref[0, pl.ds(k0, t), :]
        base = base_sc[pl.ds(k0, t), :]
        for h in range(HC):
            z_dst[h] = lax.dot_general(kb, q_head[h], _NT, preferred_element_type=F32) + slopes[h] * base

    def accumulate(j, z_src):
        k0 = block_start(j)
        alphas = []
        for h in range(HC):
            z = z_src[h]
            m_old = m_sc[h][...]
            m_new = jnp.maximum(m_old, jnp.max(z, axis=0, keepdims=True))
            alpha = jnp.exp(m_old - m_new)
            p = jnp.exp(z - m_new)
            l_sc[h][...] = alpha * l_sc[h][...] + jnp.sum(p, axis=0, keepdims=True)
            m_sc[h][...] = m_new
            p_sc[h] = p.astype(BF16)
            alphas.append(alpha)
        vt = v_ref[0, pl.ds(k0, t), :].T
        for h in range(HC):
            acc_sc[h][...] = alphas[h] * acc_sc[h][...] + jnp.dot(
                vt[h * DC:(h + 1) * DC, :], p_sc[h], preferred_element_type=F32)

    scores(0, z_sc)

    def pair_step(i, carry):
        scores(2 * i + 1, z2_sc)
        accumulate(2 * i, z_sc)

        @pl.when(2 * i + 1 <= qi)
        def _():
            scores(2 * i + 2, z_sc)
            accumulate(2 * i + 1, z2_sc)
        return carry

    lax.fori_loop(0, (qi + 2) // 2, pair_step, 0)
    heads = [acc_sc[h][...] / l_sc[h][...] for h in range(HC)]
    o_ref[0] = jnp.concatenate(heads, axis=0).T.astype(BF16)


def _dsa_attn(hm3, hik3, hw3, slopes):
    b, s, _ = hm3.shape
    t = DSA_T
    w = HC * DC
    cq, ck, cv = 9, 10, 11
    body = functools.partial(_dsa_body, n_sel=min(TOPK_MAX, s // 4), slopes=slopes)
    return pl.pallas_call(
        body,
        grid=(b, s // t),
        in_specs=[pl.BlockSpec((1, t, w), lambda bi, i: (bi, i, cq)),
                  pl.BlockSpec((1, s, w), lambda bi, i: (bi, 0, ck)),
                  pl.BlockSpec((1, s, w), lambda bi, i: (bi, 0, cv)),
                  pl.BlockSpec((1, t, HI * DI), lambda bi, i: (bi, i, 0)),
                  pl.BlockSpec((1, s, HI * DI), lambda bi, i: (bi, 0, 1)),
                  pl.BlockSpec((1, t, D_W), lambda bi, i: (bi, i, 0))],
        out_specs=pl.BlockSpec((1, t, w), lambda bi, i: (bi, i, 0)),
        out_shape=jax.ShapeDtypeStruct((b, s, w), BF16),
        scratch_shapes=([pltpu.VMEM((s, t), I32), pltpu.VMEM((s, t), I16), pltpu.VMEM((s, t), F32),
                         pltpu.VMEM((HC, t, t), F32), pltpu.VMEM((HC, t, t), F32),
                         pltpu.VMEM((HC, t, t), BF16)]
                        + [pltpu.VMEM((1, t), F32)] * (2 * HC) + [pltpu.VMEM((DC, t), F32)] * HC),
        compiler_params=_params(("parallel", "arbitrary")),
        name="dsa_attn",
    )(hm3, hm3, hm3, hik3, hik3, hw3)


def _layer_norm(y, g, b):
    mu = jnp.mean(y, axis=-1, keepdims=True)
    d = y - mu
    var = jnp.mean(d * d, axis=-1, keepdims=True)
    return d * lax.rsqrt(var + EPS) * g + b


def _out_router_body(a_ref, b_ref, c_ref, x_ref, wo_ref, g_ref, beta_ref, wr_ref, br_ref,
                     x1_ref, meta_ref, cnt_ref, *, tm):
    i = pl.program_id(0)
    wa = HA * 2 * DA
    wb = wa + HB * DB
    y = (jnp.dot(a_ref[...], wo_ref[0:wa, :], preferred_element_type=F32)
         + jnp.dot(b_ref[...], wo_ref[wa:wb, :], preferred_element_type=F32)
         + jnp.dot(c_ref[...], wo_ref[wb:, :], preferred_element_type=F32))
    x1 = _layer_norm(DEEPNORM_ALPHA * x_ref[...] + y, g_ref[...], beta_ref[...])
    x1_ref[...] = x1

    logits = lax.dot_general(wr_ref[...], x1, _NT, preferred_element_type=F32,
                             precision=lax.Precision.HIGHEST) + br_ref[...]
    row_e = lax.broadcasted_iota(I32, (N_EXPERTS, tm), 0).astype(F32)
    sel = jnp.zeros((N_EXPERTS, tm), F32)
    work = logits
    picks, vals = [], []
    for _ in range(TOP_K):
        v = jnp.max(work, axis=0, keepdims=True)
        e = jnp.min(jnp.where(work == v, row_e, float(N_EXPERTS)), axis=0, keepdims=True)
        hit = row_e == e
        sel = jnp.where(hit, 1.0, sel)
        work = jnp.where(hit, -jnp.inf, work)
        picks.append(e)
        vals.append(v)
    ex = [jnp.exp(v - vals[0]) for v in vals]
    den = ex[0] + ex[1] + ex[2] + ex[3]

    @pl.when(i == 0)
    def _():
        cnt_ref[...] = jnp.zeros_like(cnt_ref)

    earlier = jnp.where(lax.broadcasted_iota(I32, (tm, tm), 0) < lax.broadcasted_iota(I32, (tm, tm), 1),
                        1.0, 0.0).astype(BF16)
    so_far = cnt_ref[...]
    before = jnp.dot(sel.astype(BF16), earlier, preferred_element_type=F32) + so_far[:, 0:1]
    cnt_ref[...] = so_far + jnp.sum(sel, axis=1, keepdims=True)

    rows = lax.broadcasted_iota(I32, (4 * TOP_K, tm), 0)
    meta_t = jnp.zeros((4 * TOP_K, tm), F32)
    for kk in range(TOP_K):
        rank = jnp.sum(jnp.where(row_e == picks[kk], before, 0.0), axis=0, keepdims=True)
        meta_t = jnp.where(rows == kk, picks[kk], meta_t)
        meta_t = jnp.where(rows == TOP_K + kk, rank, meta_t)
        meta_t = jnp.where(rows == 2 * TOP_K + kk, ex[kk] / den, meta_t)
    meta_ref[...] = jnp.concatenate([meta_t, jnp.zeros((128 - 4 * TOP_K, tm), F32)], axis=0).T


def _out_router(oa, ob, oc, x2d, wo, g, beta, wr, br):
    t = x2d.shape[0]
    tm = 512
    row = lambda i: (i, 0)
    fix = lambda i: (0, 0)
    body = functools.partial(_out_router_body, tm=tm)
    return pl.pallas_call(
        body,
        grid=(t // tm,),
        in_specs=[pl.BlockSpec((tm, oa.shape[1]), row), pl.BlockSpec((tm, ob.shape[1]), row),
                  pl.BlockSpec((tm, oc.shape[1]), row), pl.BlockSpec((tm, D_MODEL), row),
                  pl.BlockSpec((D_MIX, D_MODEL), fix), pl.BlockSpec((1, D_MODEL), fix),
                  pl.BlockSpec((1, D_MODEL), fix), pl.BlockSpec((N_EXPERTS, D_MODEL), fix),
                  pl.BlockSpec((N_EXPERTS, 1), fix)],
        out_specs=[pl.BlockSpec((tm, D_MODEL), row),
                   pl.BlockSpec((tm, 128), row), pl.BlockSpec((N_EXPERTS, 128), fix)],
        out_shape=[jax.ShapeDtypeStruct((t, D_MODEL), F32),
                   jax.ShapeDtypeStruct((t, 128), F32), jax.ShapeDtypeStruct((N_EXPERTS, 128), F32)],
        compiler_params=_params(("arbitrary",)),
        name="out_router",
    )(oa, ob, oc, x2d, wo, g, beta, wr, br)


MOE_ROWS = 512
DISPATCH_TOKENS = 256


def _dispatch_body(dest_ref, pad_ref, x_ref, rows_out, zero_sc, sem, zero_sem):
    @pl.when(pl.program_id(0) == 0)
    def _():
        zero_sc[...] = jnp.zeros_like(zero_sc)
        n_blocks = pad_ref.shape[0]

        def zero_copy(b):
            start = pl.multiple_of(b * MOE_ROWS, MOE_ROWS)
            return pltpu.make_async_copy(zero_sc, rows_out.at[pl.ds(start, MOE_ROWS), :], zero_sem)

        def zero_start(b, carry):
            @pl.when(pad_ref[b] > 0)
            def _():
                zero_copy(b).start()
            return carry

        def zero_wait(b, carry):
            @pl.when(pad_ref[b] > 0)
            def _():
                zero_copy(b).wait()
            return carry

        lax.fori_loop(0, n_blocks, zero_start, 0)
        lax.fori_loop(0, n_blocks, zero_wait, 0)

    def token_pair(r2, carry):
        for u in range(2):
            r = 2 * r2 + u
            for kk in range(TOP_K):
                d = dest_ref[r * TOP_K + kk]
                pltpu.make_async_copy(x_ref.at[pl.ds(r, 1), :], rows_out.at[pl.ds(d, 1), :],
                                      sem).start(priority=kk % 2)
        return carry

    lax.fori_loop(0, DISPATCH_TOKENS // 2, token_pair, 0)
    for _ in range(TOP_K):
        pltpu.make_async_copy(x_ref, x_ref, sem).wait()


def _dispatch(dest_flat, block_padded, x1, n_rows):
    t = x1.shape[0]
    tm = DISPATCH_TOKENS
    return pl.pallas_call(
        _dispatch_body,
        grid=(t // tm,),
        in_specs=[pl.BlockSpec((tm * TOP_K,), lambda i: (i,), memory_space=pltpu.SMEM),
                  pl.BlockSpec(memory_space=pltpu.SMEM),
                  pl.BlockSpec((tm, D_MODEL), lambda i: (i, 0))],
        out_specs=pl.BlockSpec(memory_space=pl.ANY),
        out_shape=jax.ShapeDtypeStruct((n_rows, D_MODEL), F32),
        scratch_shapes=[pltpu.VMEM((MOE_ROWS, D_MODEL), F32), pltpu.SemaphoreType.DMA(()),
                        pltpu.SemaphoreType.DMA(())],
        compiler_params=_params(("arbitrary",)),
        name="moe_dispatch",
    )(dest_flat, block_padded, x1)


def _moe_body(be_ref, bv_ref, src_ref, x_ref, wgu_ref, bgu_ref, wd_ref, bd_ref, y_ref, wgu_sc, wd_sc):
    del src_ref
    i = pl.program_id(0)
    prev = be_ref[jnp.maximum(i - 1, 0)]

    @pl.when((i == 0) | (be_ref[i] != prev))
    def _():
        wgu_sc[...] = wgu_ref[0, 0].astype(BF16)
        wd_sc[...] = wd_ref[0, 0].astype(BF16)

    @pl.when(bv_ref[i] > 0)
    def _():
        h = jnp.dot(x_ref[...].astype(BF16), wgu_sc[...], preferred_element_type=F32) + bgu_ref[0, 0]
        gate = jnp.minimum(h[:, :D_EXPERT], SWIGLU_LIMIT)
        up = jnp.clip(h[:, D_EXPERT:], -SWIGLU_LIMIT, SWIGLU_LIMIT)
        glu = gate * (1.0 / (1.0 + jnp.exp(-SWIGLU_ALPHA * gate)))
        act = ((up + 1.0) * glu).astype(BF16)
        y_ref[...] = jnp.dot(act, wd_sc[...], preferred_element_type=F32) + bd_ref[0, 0]

    @pl.when(bv_ref[i] == 0)
    def _():
        y_ref[...] = jnp.zeros_like(y_ref)


def _moe(layer, block_expert, block_valid, block_src, x_rows, wgu, bgu, wd, bd):
    n_rows = x_rows.shape[0]
    expert = lambda i, be, bv, src: (layer, be[i], 0, 0)
    grid_spec = pltpu.PrefetchScalarGridSpec(
        num_scalar_prefetch=3,
        grid=(n_rows // MOE_ROWS,),
        in_specs=[pl.BlockSpec((MOE_ROWS, D_MODEL), lambda i, be, bv, src: (src[i], 0)),
                  pl.BlockSpec((1, 1, D_MODEL, 2 * D_EXPERT), expert),
                  pl.BlockSpec((1, 1, 1, 2 * D_EXPERT), expert),
                  pl.BlockSpec((1, 1, D_EXPERT, D_MODEL), expert),
                  pl.BlockSpec((1, 1, 1, D_MODEL), expert)],
        out_specs=pl.BlockSpec((MOE_ROWS, D_MODEL), lambda i, be, bv, src: (i, 0)),
        scratch_shapes=[pltpu.VMEM((D_MODEL, 2 * D_EXPERT), BF16), pltpu.VMEM((D_EXPERT, D_MODEL), BF16)],
    )
    return pl.pallas_call(
        _moe_body,
        grid_spec=grid_spec,
        out_shape=jax.ShapeDtypeStruct((n_rows, D_MODEL), F32),
        compiler_params=_params(("arbitrary",)),
        name="moe_experts",
    )(block_expert, block_valid, block_src, x_rows, wgu, bgu, wd, bd)


def _combine_body(x_ref, y0_ref, y1_ref, y2_ref, y3_ref, meta_ref, g_ref, beta_ref, o_ref):
    gates = meta_ref[...]
    ffn = (gates[:, 8:9] * y0_ref[...] + gates[:, 9:10] * y1_ref[...]
           + gates[:, 10:11] * y2_ref[...] + gates[:, 11:12] * y3_ref[...])
    o_ref[...] = _layer_norm(DEEPNORM_ALPHA * x_ref[...] + ffn, g_ref[...], beta_ref[...])


def _combine(x1, ys, meta, g, beta):
    t = x1.shape[0]
    tm = 512
    row = lambda i: (i, 0)
    fix = lambda i: (0, 0)
    return pl.pallas_call(
        _combine_body,
        grid=(t // tm,),
        in_specs=[pl.BlockSpec((tm, D_MODEL), row)] * 5
                 + [pl.BlockSpec((tm, 128), row), pl.BlockSpec((1, D_MODEL), fix),
                    pl.BlockSpec((1, D_MODEL), fix)],
        out_specs=pl.BlockSpec((tm, D_MODEL), row),
        out_shape=jax.ShapeDtypeStruct((t, D_MODEL), F32),
        compiler_params=_params(("parallel",)),
        name="combine_ln",
    )(x1, *ys, meta, g, beta)


def _prep_w_in(w):
    scale = DA ** -0.5
    col = jnp.ones((D_QKV,), F32)
    for lo, width in ((0, HA * 2 * DA), (3 * HA * 2 * DA, HB * DB),
                      (3 * HA * 2 * DA + 3 * HB * DB, HC * DC)):
        col = col.at[lo:lo + width].set(scale)
    main = w[:, :D_QKV] * col
    q_idx = w[:, D_QKV:D_QKV + HI * DI]
    k_idx = w[:, D_QKV + HI * DI:D_QKV + HI * DI + DI]
    w_idx = w[:, D_QKV + HI * DI + DI:] * ((HI ** -0.5) * (DI ** -0.5))
    w_pad = jnp.zeros((D_MODEL, D_W - HI), F32)
    return jnp.concatenate([main, q_idx, jnp.tile(k_idx, (1, HI)), w_idx, w_pad], axis=1).astype(BF16)


def _route(meta, cnt, n_tok):
    top_e = meta[:, 0:TOP_K].astype(I32)
    rank = meta[:, TOP_K:2 * TOP_K].astype(I32)
    counts = cnt[:, 0].astype(I32)
    n_blocks = n_tok * TOP_K // MOE_ROWS + N_EXPERTS
    blocks_e = (counts + MOE_ROWS - 1) // MOE_ROWS
    blocks_end = jnp.cumsum(blocks_e)
    row_start = (blocks_end - blocks_e) * MOE_ROWS
    experts = jnp.arange(N_EXPERTS, dtype=I32)
    start_of = jnp.sum(jnp.where(top_e[:, :, None] == experts, row_start, 0), axis=-1)
    dest = start_of + rank
    blk = jnp.arange(n_blocks, dtype=I32)
    block_expert = jnp.minimum(jnp.sum((blocks_end[None, :] <= blk[:, None]).astype(I32), axis=1),
                               N_EXPERTS - 1)
    block_valid = (blk < blocks_end[-1]).astype(I32)
    block_src = jnp.minimum(blk, blocks_end[-1] - 1)
    is_last = jnp.any((blocks_end[None, :] == blk[:, None] + 1) & (blocks_e[None, :] > 0), axis=1)
    block_padded = jnp.where(block_valid > 0, is_last, True).astype(I32)
    return dest, block_padded, n_blocks * MOE_ROWS, block_expert, block_valid, block_src


def kernel(x, w_in, lam_q1, lam_k1, lam_q2, lam_k2, subln_g, rel_bias, w_out, ln1_g, ln1_b,
           w_router, b_router, w_gu, b_gu, w_down, b_down, ln2_g, ln2_b):
    bsz, seq, _ = x.shape
    n_tok = bsz * seq
    slopes = [2.0 ** (-8.0 * i / (HA + HC)) for i in range(1, HA + HC + 1)]
    slopes_a = tuple(slopes[0::2])
    slopes_c = tuple(slopes[1::2])
    x2d = x.reshape(n_tok, D_MODEL)
    for l in range(DEPTH):
        lam_init = 0.8 - 0.6 * math.exp(-0.3 * l)
        lam = (jnp.exp(jnp.sum(lam_q1[l] * lam_k1[l])) - jnp.exp(jnp.sum(lam_q2[l] * lam_k2[l]))
               + lam_init).reshape(1).astype(F32)
        hm, hik, hw = _in_proj(x2d, _prep_w_in(w_in[l]))
        hm3 = hm.reshape(bsz, seq, D_QKV)
        out_a = _diff_attn(hm3, slopes_a, lam, subln_g[l].reshape(1, 2 * DA), lam_init)
        out_b = _band_attn(hm3, _band_bias(rel_bias[l]))
        out_c = _dsa_attn(hm3, hik.reshape(bsz, seq, D_IK), hw.reshape(bsz, seq, D_W), slopes_c)
        x1, meta, cnt = _out_router(
            out_a.reshape(n_tok, -1), out_b.reshape(n_tok, -1), out_c.reshape(n_tok, -1), x2d,
            w_out[l].astype(BF16), ln1_g[l].reshape(1, -1), ln1_b[l].reshape(1, -1),
            w_router[l].T, b_router[l].reshape(-1, 1))
        dest, block_padded, n_rows, block_expert, block_valid, block_src = _route(meta, cnt, n_tok)
        x_rows = _dispatch(dest.reshape(-1), block_padded, x1, n_rows)
        y_rows = _moe(l, block_expert, block_valid, block_src, x_rows, w_gu,
                      b_gu.reshape(DEPTH, N_EXPERTS, 1, -1), w_down, b_down.reshape(DEPTH, N_EXPERTS, 1, -1))
        ys = [y_rows[dest[:, kk]] for kk in range(TOP_K)]
        x2d = _combine(x1, ys, meta, ln2_g[l].reshape(1, -1), ln2_b[l].reshape(1, -1))
    return x2d.reshape(bsz, seq, D_MODEL)
```

```python
import functools
import math

import jax
import jax.numpy as jnp
from jax import lax
from jax.experimental import pallas as pl
from jax.experimental.pallas import tpu as pltpu

F32 = jnp.float32
BF16 = jnp.bfloat16
I32 = jnp.int32
I16 = jnp.int16

D_MODEL = 1024
DEPTH = 4
CHUNK = 64
CHUNK_SHIFT = 6
HA, DA = 4, 64
HB, DB = 4, 64
BAND_CHUNKS = 8
REL_CLIP = 128
HC, DC = 4, 64
HI, DI = 8, 32
TOPK_MAX = 256
N_EXPERTS = 32
TOP_K = 4
D_EXPERT = D_MODEL
SWIGLU_LIMIT = 7.0
SWIGLU_ALPHA = 1.702
DEEPNORM_ALPHA = (2 * DEPTH) ** 0.25
EPS = 1e-5
NEG_INF = -1e30
LOG2E = 1.4426950408889634

D_QKV = 3 * HA * 2 * DA + 3 * HB * DB + 3 * HC * DC
D_IK = 2 * HI * DI
D_W = 128
D_MIX = HA * 2 * DA + HB * DB + HC * DC

VMEM_LIMIT = 56 * 1024 * 1024

_NT = (((1,), (1,)), ((), ()))
_TN = (((0,), (0,)), ((), ()))


def _params(sem):
    return pltpu.CompilerParams(dimension_semantics=sem, vmem_limit_bytes=VMEM_LIMIT)


def _col_sum32(x):
    n, t = x.shape
    return x.reshape(n // 32, 32, t).sum(axis=0)


def _in_proj_body(x_ref, w_ref, hm_ref, hik_ref, hw_ref):
    xb = x_ref[...].astype(BF16)
    for n0 in range(0, D_QKV, 512):
        hm_ref[:, n0:n0 + 512] = jnp.dot(
            xb, w_ref[:, n0:n0 + 512], preferred_element_type=F32).astype(BF16)
    hik_ref[...] = jnp.dot(xb, w_ref[:, D_QKV:D_QKV + D_IK], preferred_element_type=F32).astype(BF16)
    hw_ref[...] = jnp.dot(xb, w_ref[:, D_QKV + D_IK:], preferred_element_type=F32)


def _in_proj(x2d, w):
    t = x2d.shape[0]
    tm = 512
    dw = D_QKV + D_IK + D_W
    return pl.pallas_call(
        _in_proj_body,
        grid=(t // tm,),
        in_specs=[pl.BlockSpec((tm, D_MODEL), lambda i: (i, 0)),
                  pl.BlockSpec((D_MODEL, dw), lambda i: (0, 0))],
        out_specs=[pl.BlockSpec((tm, D_QKV), lambda i: (i, 0)),
                   pl.BlockSpec((tm, D_IK), lambda i: (i, 0)),
                   pl.BlockSpec((tm, D_W), lambda i: (i, 0))],
        out_shape=[jax.ShapeDtypeStruct((t, D_QKV), BF16),
                   jax.ShapeDtypeStruct((t, D_IK), BF16),
                   jax.ShapeDtypeStruct((t, D_W), F32)],
        compiler_params=_params(("parallel",)),
        name="in_proj",
    )(x2d, w)


def _diff_attn_body(lam_ref, q_ref, k_ref, v_ref, g_ref, o_ref,
                    bias_sc, z_sc, z2_sc, p_sc, *state, t, slopes, out_scale):
    qi = pl.program_id(1)
    lam = lam_ref[0]
    dh = 2 * DA
    lane = lax.broadcasted_iota(I32, (1, dh), 1)
    q_half = []
    for h in range(HA):
        qh = q_ref[0, :, h * dh:(h + 1) * dh]
        zero = jnp.zeros_like(qh)
        q_half.append((jnp.where(lane < DA, qh, zero), jnp.where(lane >= DA, qh, zero)))

    @pl.when(qi == 0)
    def _():
        kk = lax.broadcasted_iota(I32, (t, t), 0)
        tt = lax.broadcasted_iota(I32, (t, t), 1)
        own_chunk = (kk >> CHUNK_SHIFT) <= (tt >> CHUNK_SHIFT)
        for h in range(HA):
            bias_sc[h] = slopes[h] * kk.astype(F32)
            bias_sc[HA + h] = jnp.where(
                own_chunk, slopes[h] * jnp.minimum(kk, 2 * tt - kk).astype(F32), NEG_INF)

    n_chain = 2 * HA
    m_sc, l_sc, acc_sc = state[:n_chain], state[n_chain:2 * n_chain], state[2 * n_chain:]
    for c in range(n_chain):
        m_sc[c][...] = jnp.full((1, t), NEG_INF, F32)
        l_sc[c][...] = jnp.zeros((1, t), F32)
        acc_sc[c][...] = jnp.zeros((dh, t), F32)

    def scores(j, z_dst):
        jb = jnp.minimum(j, qi)
        k0 = pl.multiple_of(jb * t, t)
        bias_base = jnp.where(jb == qi, HA, 0)
        for h in range(HA):
            kb = k_ref[0, pl.ds(k0, t), h * dh:(h + 1) * dh]
            bias = bias_sc[bias_base + h]
            for a in range(2):
                z_dst[2 * h + a] = lax.dot_general(
                    kb, q_half[h][a], _NT, preferred_element_type=F32) + bias

    def accumulate(j, z_src):
        k0 = pl.multiple_of(j * t, t)
        k0f = k0.astype(F32)
        alphas = []
        for c in range(n_chain):
            shift = slopes[c // 2] * k0f
            z = z_src[c]
            m_old = m_sc[c][...]
            m_new = jnp.maximum(m_old, jnp.max(z, axis=0, keepdims=True) + shift)
            alpha = jnp.exp2(m_old - m_new)
            p = jnp.exp2(z - (m_new - shift))
            l_sc[c][...] = alpha * l_sc[c][...] + jnp.sum(p, axis=0, keepdims=True)
            m_sc[c][...] = m_new
            p_sc[c] = p.astype(BF16)
            alphas.append(alpha)
        for h in range(HA):
            vt = v_ref[0, pl.ds(k0, t), h * dh:(h + 1) * dh].T
            for a in range(2):
                c = 2 * h + a
                acc_sc[c][...] = alphas[c] * acc_sc[c][...] + jnp.dot(
                    vt, p_sc[c], preferred_element_type=F32)

    scores(0, z_sc)

    def pair_step(i, carry):
        scores(2 * i + 1, z2_sc)
        accumulate(2 * i, z_sc)

        @pl.when(2 * i + 1 <= qi)
        def _():
            scores(2 * i + 2, z_sc)
            accumulate(2 * i + 1, z2_sc)
        return carry

    lax.fori_loop(0, (qi + 2) // 2, pair_step, 0)

    outs = []
    for h in range(HA):
        ot = (acc_sc[2 * h][...] / l_sc[2 * h][...]
              - lam * (acc_sc[2 * h + 1][...] / l_sc[2 * h + 1][...]))
        ms = jnp.mean(ot * ot, axis=0, keepdims=True)
        outs.append((ot * lax.rsqrt(ms + EPS)).T * g_ref[...] * out_scale)
    o_ref[0] = jnp.concatenate(outs, axis=1).astype(BF16)


def _diff_attn(hm3, slopes, lam, sub_g, lam_init):
    b, s, _ = hm3.shape
    t = 256
    dh = 2 * DA
    w = HA * dh
    body = functools.partial(_diff_attn_body, t=t, slopes=slopes, out_scale=1.0 - lam_init)
    return pl.pallas_call(
        body,
        grid=(b, s // t),
        in_specs=[pl.BlockSpec(memory_space=pltpu.SMEM),
                  pl.BlockSpec((1, t, w), lambda bi, qi: (bi, qi, 0)),
                  pl.BlockSpec((1, s, w), lambda bi, qi: (bi, 0, 1)),
                  pl.BlockSpec((1, s, w), lambda bi, qi: (bi, 0, 2)),
                  pl.BlockSpec((1, dh), lambda bi, qi: (0, 0))],
        out_specs=pl.BlockSpec((1, t, w), lambda bi, qi: (bi, qi, 0)),
        out_shape=jax.ShapeDtypeStruct((b, s, w), BF16),
        scratch_shapes=([pltpu.VMEM((2 * HA, t, t), F32), pltpu.VMEM((2 * HA, t, t), F32),
                         pltpu.VMEM((2 * HA, t, t), F32), pltpu.VMEM((2 * HA, t, t), BF16)]
                        + [pltpu.VMEM((1, t), F32)] * (4 * HA) + [pltpu.VMEM((dh, t), F32)] * (2 * HA)),
        compiler_params=_params(("parallel", "arbitrary")),
        name="diff_attn",
    )(lam, hm3, hm3, hm3, sub_g)


BAND_TQ = 256
BAND_TK = 3 * BAND_TQ


def _band_attn_body(q_ref, k0_ref, k1_ref, k2_ref, v0_ref, v1_ref, v2_ref, bias_ref, o_ref):
    i = pl.program_id(1)
    q = q_ref[0]
    k = jnp.concatenate([k0_ref[0], k1_ref[0], k2_ref[0]], axis=0)
    v = jnp.concatenate([v0_ref[0], v1_ref[0], v2_ref[0]], axis=0)
    col = lax.broadcasted_iota(I32, (1, BAND_TK), 1)
    first_valid = jnp.where(i >= 2, 0, jnp.where(i >= 1, BAND_TQ, 2 * BAND_TQ))
    kvalid = col >= first_valid
    lane = lax.broadcasted_iota(I32, (1, HB * DB), 1)
    zero = jnp.zeros_like(q)
    out = jnp.zeros((BAND_TQ, HB * DB), F32)
    for h in range(HB):
        head = (lane >> 6) == h
        s = lax.dot_general(jnp.where(head, q, zero), k, _NT, preferred_element_type=F32)
        s = jnp.where(kvalid, s + bias_ref[h], NEG_INF)
        m = jnp.max(s, axis=-1, keepdims=True)
        p = jnp.exp2(s - m)
        l = jnp.sum(p, axis=-1, keepdims=True)
        o = jnp.dot(p.astype(BF16), v, preferred_element_type=F32) / l
        out = jnp.where(head, o, out)
    o_ref[0] = out.astype(BF16)


def _band_bias(rel_bias):
    r = jnp.arange(BAND_TQ)[:, None]
    j = jnp.arange(BAND_TK)[None, :]
    qc = r // CHUNK
    kc = j // CHUNK
    in_band = (kc >= qc) & (kc <= qc + BAND_CHUNKS)
    n = BAND_TQ + BAND_TK
    m = jnp.arange(n)
    delta = jnp.where(m < BAND_TK, m, m - n)
    by_delta = rel_bias.astype(F32)[:, jnp.clip(2 * BAND_TQ - delta, -REL_CLIP, REL_CLIP) + REL_CLIP]
    flat = jnp.tile(by_delta, (1, BAND_TQ))[:, :BAND_TQ * (n - 1)]
    bias = flat.reshape(HB, BAND_TQ, n - 1)[:, :, :BAND_TK]
    return jnp.where(in_band[None], bias * LOG2E, NEG_INF)


def _band_attn(hm3, bias):
    b, s, _ = hm3.shape
    w = HB * DB
    cq, ck, cv = 6, 7, 8

    def kv_spec(col, back):
        return pl.BlockSpec((1, BAND_TQ, w), lambda bi, i: (bi, jnp.maximum(i - back, 0), col))

    return pl.pallas_call(
        _band_attn_body,
        grid=(b, s // BAND_TQ),
        in_specs=[pl.BlockSpec((1, BAND_TQ, w), lambda bi, i: (bi, i, cq)),
                  kv_spec(ck, 2), kv_spec(ck, 1), kv_spec(ck, 0),
                  kv_spec(cv, 2), kv_spec(cv, 1), kv_spec(cv, 0),
                  pl.BlockSpec((HB, BAND_TQ, BAND_TK), lambda bi, i: (0, 0, 0))],
        out_specs=pl.BlockSpec((1, BAND_TQ, w), lambda bi, i: (bi, i, 0)),
        out_shape=jax.ShapeDtypeStruct((b, s, w), BF16),
        compiler_params=_params(("parallel", "parallel")),
        name="band_attn",
    )(hm3, hm3, hm3, hm3, hm3, hm3, hm3, bias)


I16_MIN = -2 ** 15
NEG_INF_KEY = -1900671691
UNSELECTED = -1e33
DSA_T = 256


def _order_key(x):
    bits = lax.bitcast_convert_type(x, I32)
    return jnp.where(bits < 0, bits ^ jnp.int32(0x7FFFFFFF), bits)


def _dsa_body(q_ref, k_ref, v_ref, qi_ref, ki_ref, w_ref, o_ref, key_sc, half_sc, base_sc, z_sc, z2_sc,
              p_sc, *state, n_sel, slopes):
    t = DSA_T
    qi = pl.program_id(1)
    q0 = qi * t
    nblk = qi + 1
    kk = lax.broadcasted_iota(I32, (t, t), 0)
    tg = q0 + lax.broadcasted_iota(I32, (t, t), 1)

    def block_start(j):
        return pl.multiple_of(j * t, t)

    qidx = qi_ref[0]
    lane_i = lax.broadcasted_iota(I32, (1, HI * DI), 1)
    zero_i = jnp.zeros_like(qidx)
    qm = [jnp.where((lane_i >> 5) == h, qidx, zero_i) for h in range(HI)]
    wt = w_ref[0].T

    def score_step(j, carry):
        k0 = block_start(j)
        kib = ki_ref[0, pl.ds(k0, t), :]
        sc = jnp.zeros((t, t), F32)
        for h in range(HI):
            rel = jnp.maximum(lax.dot_general(kib, qm[h], _NT, preferred_element_type=F32), 0.0)
            sc = sc + wt[h:h + 1, :] * rel
        allowed = ((k0 + kk) >> CHUNK_SHIFT) <= (tg >> CHUNK_SHIFT)
        key = _order_key(jnp.where(allowed, sc, NEG_INF))
        key_sc[pl.ds(k0, t), :] = key
        half_sc[pl.ds(k0, t), :] = (key >> 16).astype(I16)
        return carry

    lax.fori_loop(0, nblk, score_step, 0)

    kf = float(n_sel)

    def count_half(cmp):
        def blk(j, acc):
            hit = jnp.where(cmp(half_sc[pl.ds(block_start(j), t), :]), jnp.int16(1), jnp.int16(0))
            for g in range(t // 32):
                acc = acc + hit[g * 32:(g + 1) * 32, :]
            return acc
        acc = lax.fori_loop(0, nblk, blk, jnp.zeros((32, t), I16))
        return jnp.sum(acc.astype(F32), axis=0, keepdims=True)

    def search_half(need):
        def count_ge(cand):
            c16 = cand.astype(I16)
            return count_half(lambda x: x >= c16)

        best = jnp.where(count_ge(jnp.zeros((1, t), I32)) >= need, 0, I16_MIN).astype(I32)

        def bit_step(i, best):
            cand = best + jnp.left_shift(jnp.int32(1), 14 - i)
            return jnp.where(count_ge(cand) >= need, cand, best)

        return lax.fori_loop(0, 15, bit_step, best)

    upper = search_half(kf)
    upper16 = upper.astype(I16)
    n_above = count_half(lambda x: x > upper16)

    def lower_step(j, carry):
        k0 = block_start(j)
        key = key_sc[pl.ds(k0, t), :]
        low = (key & 0xFFFF) + I16_MIN
        half_sc[pl.ds(k0, t), :] = jnp.where((key >> 16) == upper, low, I16_MIN).astype(I16)
        return carry

    lax.fori_loop(0, nblk, lower_step, 0)
    lower = search_half(kf - n_above)
    thr = upper * 65536 + (lower - I16_MIN)

    def alibi_or_unselected(keep, k0):
        kg = k0 + kk
        allowed = (kg >> CHUNK_SHIFT) <= (tg >> CHUNK_SHIFT)
        return jnp.where(keep, jnp.where(allowed, -jnp.abs(tg - kg).astype(F32), UNSELECTED), UNSELECTED)

    def select_step(j, n_ge):
        k0 = block_start(j)
        ge = key_sc[pl.ds(k0, t), :] >= thr
        base_sc[pl.ds(k0, t), :] = alibi_or_unselected(ge, k0)
        return n_ge + _col_sum32(jnp.where(ge, 1.0, 0.0))

    n_ge = jnp.sum(lax.fori_loop(0, nblk, select_step, jnp.zeros((32, t), F32)), axis=0, keepdims=True)
    over = jnp.where(n_ge > kf, jnp.where(thr != NEG_INF_KEY, 1.0, 0.0), 0.0)

    @pl.when(jnp.max(over) > 0.0)
    def _():
        def gt_step(j, acc):
            kb = key_sc[pl.ds(block_start(j), t), :]
            return acc + _col_sum32(jnp.where(kb > thr, 1.0, 0.0))
        n_gt = jnp.sum(lax.fori_loop(0, nblk, gt_step, jnp.zeros((32, t), F32)), axis=0, keepdims=True)
        need = kf - n_gt
        earlier = jnp.where(lax.broadcasted_iota(I32, (t, t), 1) < kk, 1.0, 0.0).astype(BF16)

        def tie_step(j, offs):
            k0 = block_start(j)
            kb = key_sc[pl.ds(k0, t), :]
            e = jnp.where(kb == thr, 1.0, 0.0)
            rank = jnp.dot(earlier, e.astype(BF16), preferred_element_type=F32) + offs
            keep = jnp.where(kb > thr, 1.0, jnp.where(rank < need, e, 0.0))
            base_sc[pl.ds(k0, t), :] = alibi_or_unselected(keep > 0.0, k0)
            return offs + jnp.sum(e, axis=0, keepdims=True)

        lax.fori_loop(0, nblk, tie_step, jnp.zeros((1, t), F32))

    q = q_ref[0]
    lane = lax.broadcasted_iota(I32, (1, HC * DC), 1)
    zero = jnp.zeros_like(q)
    q_head = [jnp.where((lane >> 6) == h, q, zero) for h in range(HC)]
    m_sc, l_sc, acc_sc = state[:HC], state[HC:2 * HC], state[2 * HC:]
    for h in range(HC):
        m_sc[h][...] = jnp.full((1, t), NEG_INF, F32)
        l_sc[h][...] = jnp.zeros((1, t), F32)
        acc_sc[h][...] = jnp.zeros((DC, t), F32)

    def scores(j, z_dst):
        k0 = block_start(jnp.minimum(j, qi))
        kb = k_ref[0, pl.ds(k0, t), :]
        base = base_sc[pl.ds(k0, t), :]
        for h in range(HC):
            z_dst[h] = lax.dot_general(kb, q_head[h], _NT, preferred_element_type=F32) + slopes[h] * base

    def accumulate(j, z_src):
        k0 = block_start(j)
        alphas = []
        for h in range(HC):
            z = z_src[h]
            m_old = m_sc[h][...]
            m_new = jnp.maximum(m_old, jnp.max(z, axis=0, keepdims=True))
            alpha = jnp.exp2(m_old - m_new)
            p = jnp.exp2(z - m_new)
            l_sc[h][...] = alpha * l_sc[h][...] + jnp.sum(p, axis=0, keepdims=True)
            m_sc[h][...] = m_new
            p_sc[h] = p.astype(BF16)
            alphas.append(alpha)
        vt = v_ref[0, pl.ds(k0, t), :].T
        for h in range(HC):
            acc_sc[h][...] = alphas[h] * acc_sc[h][...] + jnp.dot(
                vt[h * DC:(h + 1) * DC, :], p_sc[h], preferred_element_type=F32)

    scores(0, z_sc)

    def pair_step(i, carry):
        scores(2 * i + 1, z2_sc)
        accumulate(2 * i, z_sc)

        @pl.when(2 * i + 1 <= qi)
        def _():
            scores(2 * i + 2, z_sc)
            accumulate(2 * i + 1, z2_sc)
        return carry

    lax.fori_loop(0, (qi + 2) // 2, pair_step, 0)
    heads = [acc_sc[h][...] / l_sc[h][...] for h in range(HC)]
    o_ref[0] = jnp.concatenate(heads, axis=0).T.astype(BF16)


def _dsa_attn(hm3, hik3, hw3, slopes):
    b, s, _ = hm3.shape
    t = DSA_T
    w = HC * DC
    cq, ck, cv = 9, 10, 11
    body = functools.partial(_dsa_body, n_sel=min(TOPK_MAX, s // 4), slopes=slopes)
    return pl.pallas_call(
        body,
        grid=(b, s // t),
        in_specs=[pl.BlockSpec((1, t, w), lambda bi, i: (bi, i, cq)),
                  pl.BlockSpec((1, s, w), lambda bi, i: (bi, 0, ck)),
                  pl.BlockSpec((1, s, w), lambda bi, i: (bi, 0, cv)),
                  pl.BlockSpec((1, t, HI * DI), lambda bi, i: (bi, i, 0)),
                  pl.BlockSpec((1, s, HI * DI), lambda bi, i: (bi, 0, 1)),
                  pl.BlockSpec((1, t, D_W), lambda bi, i: (bi, i, 0))],
        out_specs=pl.BlockSpec((1, t, w), lambda bi, i: (bi, i, 0)),
        out_shape=jax.ShapeDtypeStruct((b, s, w), BF16),
        scratch_shapes=([pltpu.VMEM((s, t), I32), pltpu.VMEM((s, t), I16), pltpu.VMEM((s, t), F32),
                         pltpu.VMEM((HC, t, t), F32), pltpu.VMEM((HC, t, t), F32),
                         pltpu.VMEM((HC, t, t), BF16)]
                        + [pltpu.VMEM((1, t), F32)] * (2 * HC) + [pltpu.VMEM((DC, t), F32)] * HC),
        compiler_params=_params(("parallel", "arbitrary")),
        name="dsa_attn",
    )(hm3, hm3, hm3, hik3, hik3, hw3)


def _layer_norm(y, g, b):
    mu = jnp.mean(y, axis=-1, keepdims=True)
    d = y - mu
    var = jnp.mean(d * d, axis=-1, keepdims=True)
    return d * lax.rsqrt(var + EPS) * g + b


def _out_router_body(a_ref, b_ref, c_ref, x_ref, wo_ref, g_ref, beta_ref, wr_ref, br_ref,
                     x1_ref, meta_ref, cnt_ref, *, tm):
    i = pl.program_id(0)
    wa = HA * 2 * DA
    wb = wa + HB * DB
    y = (jnp.dot(a_ref[...], wo_ref[0:wa, :], preferred_element_type=F32)
         + jnp.dot(b_ref[...], wo_ref[wa:wb, :], preferred_element_type=F32)
         + jnp.dot(c_ref[...], wo_ref[wb:, :], preferred_element_type=F32))
    x1 = _layer_norm(DEEPNORM_ALPHA * x_ref[...] + y, g_ref[...], beta_ref[...])
    x1_ref[...] = x1

    logits = lax.dot_general(wr_ref[...], x1, _NT, preferred_element_type=F32,
                             precision=lax.Precision.HIGHEST) + br_ref[...]
    row_e = lax.broadcasted_iota(I32, (N_EXPERTS, tm), 0).astype(F32)
    sel = jnp.zeros((N_EXPERTS, tm), F32)
    work = logits
    picks, vals = [], []
    for _ in range(TOP_K):
        v = jnp.max(work, axis=0, keepdims=True)
        e = jnp.min(jnp.where(work == v, row_e, float(N_EXPERTS)), axis=0, keepdims=True)
        hit = row_e == e
        sel = jnp.where(hit, 1.0, sel)
        work = jnp.where(hit, -jnp.inf, work)
        picks.append(e)
        vals.append(v)
    ex = [jnp.exp(v - vals[0]) for v in vals]
    den = ex[0] + ex[1] + ex[2] + ex[3]

    @pl.when(i == 0)
    def _():
        cnt_ref[...] = jnp.zeros_like(cnt_ref)

    earlier = jnp.where(lax.broadcasted_iota(I32, (tm, tm), 0) < lax.broadcasted_iota(I32, (tm, tm), 1),
                        1.0, 0.0).astype(BF16)
    so_far = cnt_ref[...]
    before = jnp.dot(sel.astype(BF16), earlier, preferred_element_type=F32) + so_far[:, 0:1]
    cnt_ref[...] = so_far + jnp.sum(sel, axis=1, keepdims=True)

    rows = lax.broadcasted_iota(I32, (4 * TOP_K, tm), 0)
    meta_t = jnp.zeros((4 * TOP_K, tm), F32)
    for kk in range(TOP_K):
        rank = jnp.sum(jnp.where(row_e == picks[kk], before, 0.0), axis=0, keepdims=True)
        meta_t = jnp.where(rows == kk, picks[kk], meta_t)
        meta_t = jnp.where(rows == TOP_K + kk, rank, meta_t)
        meta_t = jnp.where(rows == 2 * TOP_K + kk, ex[kk] / den, meta_t)
    meta_ref[...] = jnp.concatenate([meta_t, jnp.zeros((128 - 4 * TOP_K, tm), F32)], axis=0).T


def _out_router(oa, ob, oc, x2d, wo, g, beta, wr, br):
    t = x2d.shape[0]
    tm = 512
    row = lambda i: (i, 0)
    fix = lambda i: (0, 0)
    body = functools.partial(_out_router_body, tm=tm)
    return pl.pallas_call(
        body,
        grid=(t // tm,),
        in_specs=[pl.BlockSpec((tm, oa.shape[1]), row), pl.BlockSpec((tm, ob.shape[1]), row),
                  pl.BlockSpec((tm, oc.shape[1]), row), pl.BlockSpec((tm, D_MODEL), row),
                  pl.BlockSpec((D_MIX, D_MODEL), fix), pl.BlockSpec((1, D_MODEL), fix),
                  pl.BlockSpec((1, D_MODEL), fix), pl.BlockSpec((N_EXPERTS, D_MODEL), fix),
                  pl.BlockSpec((N_EXPERTS, 1), fix)],
        out_specs=[pl.BlockSpec((tm, D_MODEL), row),
                   pl.BlockSpec((tm, 128), row), pl.BlockSpec((N_EXPERTS, 128), fix)],
        out_shape=[jax.ShapeDtypeStruct((t, D_MODEL), F32),
                   jax.ShapeDtypeStruct((t, 128), F32), jax.ShapeDtypeStruct((N_EXPERTS, 128), F32)],
        compiler_params=_params(("arbitrary",)),
        name="out_router",
    )(oa, ob, oc, x2d, wo, g, beta, wr, br)


MOE_ROWS = 512
DISPATCH_TOKENS = 512


def _dispatch_body(dest_ref, pad_ref, x_ref, rows_out, zero_sc, sem, zero_sem):
    @pl.when(pl.program_id(0) == 0)
    def _():
        zero_sc[...] = jnp.zeros_like(zero_sc)
        n_blocks = pad_ref.shape[0]

        def zero_copy(b):
            start = pl.multiple_of(b * MOE_ROWS, MOE_ROWS)
            return pltpu.make_async_copy(zero_sc, rows_out.at[pl.ds(start, MOE_ROWS), :], zero_sem)

        def zero_start(b, carry):
            @pl.when(pad_ref[b] > 0)
            def _():
                zero_copy(b).start()
            return carry

        def zero_wait(b, carry):
            @pl.when(pad_ref[b] > 0)
            def _():
                zero_copy(b).wait()
            return carry

        lax.fori_loop(0, n_blocks, zero_start, 0)
        lax.fori_loop(0, n_blocks, zero_wait, 0)

    def token_pair(r2, carry):
        for u in range(2):
            r = 2 * r2 + u
            for kk in range(TOP_K):
                d = dest_ref[r * TOP_K + kk]
                pltpu.make_async_copy(x_ref.at[pl.ds(r, 1), :], rows_out.at[pl.ds(d, 1), :],
                                      sem).start(priority=kk % 2)
        return carry

    lax.fori_loop(0, DISPATCH_TOKENS // 2, token_pair, 0)
    for _ in range(TOP_K):
        pltpu.make_async_copy(x_ref, x_ref, sem).wait()


def _dispatch(dest_flat, block_padded, x1, n_rows):
    t = x1.shape[0]
    tm = DISPATCH_TOKENS
    return pl.pallas_call(
        _dispatch_body,
        grid=(t // tm,),
        in_specs=[pl.BlockSpec((tm * TOP_K,), lambda i: (i,), memory_space=pltpu.SMEM),
                  pl.BlockSpec(memory_space=pltpu.SMEM),
                  pl.BlockSpec((tm, D_MODEL), lambda i: (i, 0))],
        out_specs=pl.BlockSpec(memory_space=pl.ANY),
        out_shape=jax.ShapeDtypeStruct((n_rows, D_MODEL), F32),
        scratch_shapes=[pltpu.VMEM((MOE_ROWS, D_MODEL), F32), pltpu.SemaphoreType.DMA(()),
                        pltpu.SemaphoreType.DMA(())],
        compiler_params=_params(("arbitrary",)),
        name="moe_dispatch",
    )(dest_flat, block_padded, x1)


def _moe_body(be_ref, bv_ref, src_ref, x_ref, wgu_ref, bgu_ref, wd_ref, bd_ref, y_ref, wgu_sc, wd_sc):
    del src_ref
    i = pl.program_id(0)
    prev = be_ref[jnp.maximum(i - 1, 0)]

    @pl.when((i == 0) | (be_ref[i] != prev))
    def _():
        wgu_sc[...] = wgu_ref[0, 0].astype(BF16)
        wd_sc[...] = wd_ref[0, 0].astype(BF16)

    @pl.when(bv_ref[i] > 0)
    def _():
        h = jnp.dot(x_ref[...].astype(BF16), wgu_sc[...], preferred_element_type=F32) + bgu_ref[0, 0]
        gate = jnp.minimum(h[:, :D_EXPERT], SWIGLU_LIMIT)
        up = jnp.clip(h[:, D_EXPERT:], -SWIGLU_LIMIT, SWIGLU_LIMIT)
        glu = gate * (1.0 / (1.0 + jnp.exp(-SWIGLU_ALPHA * gate)))
        act = ((up + 1.0) * glu).astype(BF16)
        y_ref[...] = jnp.dot(act, wd_sc[...], preferred_element_type=F32) + bd_ref[0, 0]

    @pl.when(bv_ref[i] == 0)
    def _():
        y_ref[...] = jnp.zeros_like(y_ref)


def _moe(layer, block_expert, block_valid, block_src, x_rows, wgu, bgu, wd, bd):
    n_rows = x_rows.shape[0]
    expert = lambda i, be, bv, src: (layer, be[i], 0, 0)
    grid_spec = pltpu.PrefetchScalarGridSpec(
        num_scalar_prefetch=3,
        grid=(n_rows // MOE_ROWS,),
        in_specs=[pl.BlockSpec((MOE_ROWS, D_MODEL), lambda i, be, bv, src: (src[i], 0)),
                  pl.BlockSpec((1, 1, D_MODEL, 2 * D_EXPERT), expert),
                  pl.BlockSpec((1, 1, 1, 2 * D_EXPERT), expert),
                  pl.BlockSpec((1, 1, D_EXPERT, D_MODEL), expert),
                  pl.BlockSpec((1, 1, 1, D_MODEL), expert)],
        out_specs=pl.BlockSpec((MOE_ROWS, D_MODEL), lambda i, be, bv, src: (i, 0)),
        scratch_shapes=[pltpu.VMEM((D_MODEL, 2 * D_EXPERT), BF16), pltpu.VMEM((D_EXPERT, D_MODEL), BF16)],
    )
    return pl.pallas_call(
        _moe_body,
        grid_spec=grid_spec,
        out_shape=jax.ShapeDtypeStruct((n_rows, D_MODEL), F32),
        compiler_params=_params(("arbitrary",)),
        name="moe_experts",
    )(block_expert, block_valid, block_src, x_rows, wgu, bgu, wd, bd)


def _combine_body(x_ref, y0_ref, y1_ref, y2_ref, y3_ref, meta_ref, g_ref, beta_ref, o_ref):
    gates = meta_ref[...]
    ffn = (gates[:, 8:9] * y0_ref[0] + gates[:, 9:10] * y1_ref[0]
           + gates[:, 10:11] * y2_ref[0] + gates[:, 11:12] * y3_ref[0])
    o_ref[...] = _layer_norm(DEEPNORM_ALPHA * x_ref[...] + ffn, g_ref[...], beta_ref[...])


def _combine(x1, ys, meta, g, beta):
    t = x1.shape[0]
    tm = 512
    row = lambda i: (i, 0)
    fix = lambda i: (0, 0)
    slot = lambda kk: pl.BlockSpec((1, tm, D_MODEL), lambda i: (kk, i, 0))
    return pl.pallas_call(
        _combine_body,
        grid=(t // tm,),
        in_specs=[pl.BlockSpec((tm, D_MODEL), row)] + [slot(kk) for kk in range(TOP_K)]
                 + [pl.BlockSpec((tm, 128), row), pl.BlockSpec((1, D_MODEL), fix),
                    pl.BlockSpec((1, D_MODEL), fix)],
        out_specs=pl.BlockSpec((tm, D_MODEL), row),
        out_shape=jax.ShapeDtypeStruct((t, D_MODEL), F32),
        compiler_params=_params(("parallel",)),
        name="combine_ln",
    )(x1, ys, ys, ys, ys, meta, g, beta)


def _prep_w_in(w):
    scale = DA ** -0.5 * LOG2E
    col = jnp.ones((D_QKV,), F32)
    for lo, width in ((0, HA * 2 * DA), (3 * HA * 2 * DA, HB * DB),
                      (3 * HA * 2 * DA + 3 * HB * DB, HC * DC)):
        col = col.at[lo:lo + width].set(scale)
    main = w[:, :D_QKV] * col
    q_idx = w[:, D_QKV:D_QKV + HI * DI]
    k_idx = w[:, D_QKV + HI * DI:D_QKV + HI * DI + DI]
    w_idx = w[:, D_QKV + HI * DI + DI:] * ((HI ** -0.5) * (DI ** -0.5))
    w_pad = jnp.zeros((D_MODEL, D_W - HI), F32)
    return jnp.concatenate([main, q_idx, jnp.tile(k_idx, (1, HI)), w_idx, w_pad], axis=1).astype(BF16)


def _route(meta, cnt, n_tok):
    top_e = meta[:, 0:TOP_K].astype(I32)
    rank = meta[:, TOP_K:2 * TOP_K].astype(I32)
    counts = cnt[:, 0].astype(I32)
    n_blocks = n_tok * TOP_K // MOE_ROWS + N_EXPERTS
    blocks_e = (counts + MOE_ROWS - 1) // MOE_ROWS
    blocks_end = jnp.cumsum(blocks_e)
    row_start = (blocks_end - blocks_e) * MOE_ROWS
    experts = jnp.arange(N_EXPERTS, dtype=I32)
    start_of = jnp.sum(jnp.where(top_e[:, :, None] == experts, row_start, 0), axis=-1)
    dest = start_of + rank
    blk = jnp.arange(n_blocks, dtype=I32)
    block_expert = jnp.minimum(jnp.sum((blocks_end[None, :] <= blk[:, None]).astype(I32), axis=1),
                               N_EXPERTS - 1)
    block_valid = (blk < blocks_end[-1]).astype(I32)
    block_src = jnp.minimum(blk, blocks_end[-1] - 1)
    is_last = jnp.any((blocks_end[None, :] == blk[:, None] + 1) & (blocks_e[None, :] > 0), axis=1)
    block_padded = jnp.where(block_valid > 0, is_last, True).astype(I32)
    return dest, block_padded, n_blocks * MOE_ROWS, block_expert, block_valid, block_src


def kernel(x, w_in, lam_q1, lam_k1, lam_q2, lam_k2, subln_g, rel_bias, w_out, ln1_g, ln1_b,
           w_router, b_router, w_gu, b_gu, w_down, b_down, ln2_g, ln2_b):
    bsz, seq, _ = x.shape
    n_tok = bsz * seq
    slopes = [2.0 ** (-8.0 * i / (HA + HC)) for i in range(1, HA + HC + 1)]
    slopes_a = tuple(s * LOG2E for s in slopes[0::2])
    slopes_c = tuple(s * LOG2E for s in slopes[1::2])
    x2d = x.reshape(n_tok, D_MODEL)
    for l in range(DEPTH):
        lam_init = 0.8 - 0.6 * math.exp(-0.3 * l)
        lam = (jnp.exp(jnp.sum(lam_q1[l] * lam_k1[l])) - jnp.exp(jnp.sum(lam_q2[l] * lam_k2[l]))
               + lam_init).reshape(1).astype(F32)
        hm, hik, hw = _in_proj(x2d, _prep_w_in(w_in[l]))
        hm3 = hm.reshape(bsz, seq, D_QKV)
        out_a = _diff_attn(hm3, slopes_a, lam, subln_g[l].reshape(1, 2 * DA), lam_init)
        out_b = _band_attn(hm3, _band_bias(rel_bias[l]))
        out_c = _dsa_attn(hm3, hik.reshape(bsz, seq, D_IK), hw.reshape(bsz, seq, D_W), slopes_c)
        x1, meta, cnt = _out_router(
            out_a.reshape(n_tok, -1), out_b.reshape(n_tok, -1), out_c.reshape(n_tok, -1), x2d,
            w_out[l].astype(BF16), ln1_g[l].reshape(1, -1), ln1_b[l].reshape(1, -1),
            w_router[l].T, b_router[l].reshape(-1, 1))
        dest, block_padded, n_rows, block_expert, block_valid, block_src = _route(meta, cnt, n_tok)
        x_rows = _dispatch(dest.reshape(-1), block_padded, x1, n_rows)
        y_rows = _moe(l, block_expert, block_valid, block_src, x_rows, w_gu,
                      b_gu.reshape(DEPTH, N_EXPERTS, 1, -1), w_down, b_down.reshape(DEPTH, N_EXPERTS, 1, -1))
        ys = y_rows[dest.T]
        x2d = _combine(x1, ys, meta, ln2_g[l].reshape(1, -1), ln2_b[l].reshape(1, -1))
    return x2d.reshape(bsz, seq, D_MODEL)
```

```python
import functools
import math

import jax
import jax.numpy as jnp
from jax import lax
from jax.experimental import pallas as pl
from jax.experimental.pallas import tpu as pltpu

F32 = jnp.float32
BF16 = jnp.bfloat16
I32 = jnp.int32
I16 = jnp.int16

D_MODEL = 1024
DEPTH = 4
CHUNK = 64
CHUNK_SHIFT = 6
HA, DA = 4, 64
HB, DB = 4, 64
BAND_CHUNKS = 8
REL_CLIP = 128
HC, DC = 4, 64
HI, DI = 8, 32
TOPK_MAX = 256
N_EXPERTS = 32
TOP_K = 4
D_EXPERT = D_MODEL
SWIGLU_LIMIT = 7.0
SWIGLU_ALPHA = 1.702
DEEPNORM_ALPHA = (2 * DEPTH) ** 0.25
EPS = 1e-5
NEG_INF = -1e30
LOG2E = 1.4426950408889634

D_QKV = 3 * HA * 2 * DA + 3 * HB * DB + 3 * HC * DC
D_IK = 2 * HI * DI
D_W = 128
D_MIX = HA * 2 * DA + HB * DB + HC * DC

VMEM_LIMIT = 56 * 1024 * 1024

_NT = (((1,), (1,)), ((), ()))
_TN = (((0,), (0,)), ((), ()))


def _params(sem):
    return pltpu.CompilerParams(dimension_semantics=sem, vmem_limit_bytes=VMEM_LIMIT)


def _col_sum32(x):
    n, t = x.shape
    return x.reshape(n // 32, 32, t).sum(axis=0)


def _in_proj_body(x_ref, w_ref, hm_ref, hik_ref, hw_ref):
    xb = x_ref[...].astype(BF16)
    for n0 in range(0, D_QKV, 512):
        hm_ref[:, n0:n0 + 512] = jnp.dot(
            xb, w_ref[:, n0:n0 + 512], preferred_element_type=F32).astype(BF16)
    hik_ref[...] = jnp.dot(xb, w_ref[:, D_QKV:D_QKV + D_IK], preferred_element_type=F32).astype(BF16)
    hw_ref[...] = jnp.dot(xb, w_ref[:, D_QKV + D_IK:], preferred_element_type=F32)


def _in_proj(x2d, w):
    t = x2d.shape[0]
    tm = 512
    dw = D_QKV + D_IK + D_W
    return pl.pallas_call(
        _in_proj_body,
        grid=(t // tm,),
        in_specs=[pl.BlockSpec((tm, D_MODEL), lambda i: (i, 0)),
                  pl.BlockSpec((D_MODEL, dw), lambda i: (0, 0))],
        out_specs=[pl.BlockSpec((tm, D_QKV), lambda i: (i, 0)),
                   pl.BlockSpec((tm, D_IK), lambda i: (i, 0)),
                   pl.BlockSpec((tm, D_W), lambda i: (i, 0))],
        out_shape=[jax.ShapeDtypeStruct((t, D_QKV), BF16),
                   jax.ShapeDtypeStruct((t, D_IK), BF16),
                   jax.ShapeDtypeStruct((t, D_W), F32)],
        compiler_params=_params(("parallel",)),
        name="in_proj",
    )(x2d, w)


def _diff_attn_body(lam_ref, q_ref, k_ref, v_ref, g_ref, o_ref,
                    bias_sc, z_sc, z2_sc, p_sc, *state, t, slopes, out_scale):
    qi = pl.program_id(1)
    lam = lam_ref[0]
    dh = 2 * DA
    lane = lax.broadcasted_iota(I32, (1, dh), 1)
    q_half = []
    for h in range(HA):
        qh = q_ref[0, :, h * dh:(h + 1) * dh]
        zero = jnp.zeros_like(qh)
        q_half.append((jnp.where(lane < DA, qh, zero), jnp.where(lane >= DA, qh, zero)))

    @pl.when(qi == 0)
    def _():
        kk = lax.broadcasted_iota(I32, (t, t), 0)
        tt = lax.broadcasted_iota(I32, (t, t), 1)
        own_chunk = (kk >> CHUNK_SHIFT) <= (tt >> CHUNK_SHIFT)
        for h in range(HA):
            bias_sc[h] = slopes[h] * kk.astype(F32)
            bias_sc[HA + h] = jnp.where(
                own_chunk, slopes[h] * jnp.minimum(kk, 2 * tt - kk).astype(F32), NEG_INF)

    n_chain = 2 * HA
    m_sc, l_sc, acc_sc = state[:n_chain], state[n_chain:2 * n_chain], state[2 * n_chain:]
    for c in range(n_chain):
        m_sc[c][...] = jnp.full((1, t), NEG_INF, F32)
        l_sc[c][...] = jnp.zeros((1, t), F32)
        acc_sc[c][...] = jnp.zeros((dh, t), F32)

    def scores(j, z_dst):
        jb = jnp.minimum(j, qi)
        k0 = pl.multiple_of(jb * t, t)
        bias_base = jnp.where(jb == qi, HA, 0)
        for h in range(HA):
            kb = k_ref[0, pl.ds(k0, t), h * dh:(h + 1) * dh]
            bias = bias_sc[bias_base + h]
            for a in range(2):
                z_dst[2 * h + a] = lax.dot_general(
                    kb, q_half[h][a], _NT, preferred_element_type=F32) + bias

    def accumulate(j, z_src):
        k0 = pl.multiple_of(j * t, t)
        k0f = k0.astype(F32)
        alphas = []
        for c in range(n_chain):
            shift = slopes[c // 2] * k0f
            z = z_src[c]
            m_old = m_sc[c][...]
            m_new = jnp.maximum(m_old, jnp.max(z, axis=0, keepdims=True) + shift)
            alpha = jnp.exp2(m_old - m_new)
            p = jnp.exp2(z - (m_new - shift))
            l_sc[c][...] = alpha * l_sc[c][...] + jnp.sum(p, axis=0, keepdims=True)
            m_sc[c][...] = m_new
            p_sc[c] = p.astype(BF16)
            alphas.append(alpha)
        for h in range(HA):
            vt = v_ref[0, pl.ds(k0, t), h * dh:(h + 1) * dh].T
            for a in range(2):
                c = 2 * h + a
                acc_sc[c][...] = alphas[c] * acc_sc[c][...] + jnp.dot(
                    vt, p_sc[c], preferred_element_type=F32)

    scores(0, z_sc)

    def pair_step(i, carry):
        scores(2 * i + 1, z2_sc)
        accumulate(2 * i, z_sc)

        @pl.when(2 * i + 1 <= qi)
        def _():
            scores(2 * i + 2, z_sc)
            accumulate(2 * i + 1, z2_sc)
        return carry

    lax.fori_loop(0, (qi + 2) // 2, pair_step, 0)

    outs = []
    for h in range(HA):
        ot = (acc_sc[2 * h][...] / l_sc[2 * h][...]
              - lam * (acc_sc[2 * h + 1][...] / l_sc[2 * h + 1][...]))
        ms = jnp.mean(ot * ot, axis=0, keepdims=True)
        outs.append((ot * lax.rsqrt(ms + EPS)).T * g_ref[...] * out_scale)
    o_ref[0] = jnp.concatenate(outs, axis=1).astype(BF16)


def _diff_attn(hm3, slopes, lam, sub_g, lam_init):
    b, s, _ = hm3.shape
    t = 256
    dh = 2 * DA
    w = HA * dh
    body = functools.partial(_diff_attn_body, t=t, slopes=slopes, out_scale=1.0 - lam_init)
    return pl.pallas_call(
        body,
        grid=(b, s // t),
        in_specs=[pl.BlockSpec(memory_space=pltpu.SMEM),
                  pl.BlockSpec((1, t, w), lambda bi, qi: (bi, qi, 0)),
                  pl.BlockSpec((1, s, w), lambda bi, qi: (bi, 0, 1)),
                  pl.BlockSpec((1, s, w), lambda bi, qi: (bi, 0, 2)),
                  pl.BlockSpec((1, dh), lambda bi, qi: (0, 0))],
        out_specs=pl.BlockSpec((1, t, w), lambda bi, qi: (bi, qi, 0)),
        out_shape=jax.ShapeDtypeStruct((b, s, w), BF16),
        scratch_shapes=([pltpu.VMEM((2 * HA, t, t), F32), pltpu.VMEM((2 * HA, t, t), F32),
                         pltpu.VMEM((2 * HA, t, t), F32), pltpu.VMEM((2 * HA, t, t), BF16)]
                        + [pltpu.VMEM((1, t), F32)] * (4 * HA) + [pltpu.VMEM((dh, t), F32)] * (2 * HA)),
        compiler_params=_params(("parallel", "arbitrary")),
        name="diff_attn",
    )(lam, hm3, hm3, hm3, sub_g)


BAND_TQ = 256
BAND_TK = 3 * BAND_TQ


def _band_attn_body(q_ref, k0_ref, k1_ref, k2_ref, v0_ref, v1_ref, v2_ref, bias_ref, o_ref, s_sc, p_sc):
    i = pl.program_id(1)
    q = q_ref[0]
    k = jnp.concatenate([k0_ref[0], k1_ref[0], k2_ref[0]], axis=0)
    v = jnp.concatenate([v0_ref[0], v1_ref[0], v2_ref[0]], axis=0)
    col = lax.broadcasted_iota(I32, (1, BAND_TK), 1)
    first_valid = jnp.where(i >= 2, 0, jnp.where(i >= 1, BAND_TQ, 2 * BAND_TQ))
    kvalid = col >= first_valid
    lane = lax.broadcasted_iota(I32, (1, HB * DB), 1)
    zero = jnp.zeros_like(q)
    heads = [(lane >> 6) == h for h in range(HB)]
    for h in range(HB):
        s = lax.dot_general(jnp.where(heads[h], q, zero), k, _NT, preferred_element_type=F32)
        s_sc[h] = jnp.where(kvalid, s + bias_ref[h], NEG_INF)
    sums = []
    for h in range(HB):
        s = s_sc[h]
        p = jnp.exp2(s - jnp.max(s, axis=-1, keepdims=True))
        sums.append(jnp.sum(p, axis=-1, keepdims=True))
        p_sc[h] = p.astype(BF16)
    out = jnp.zeros((BAND_TQ, HB * DB), F32)
    for h in range(HB):
        o = jnp.dot(p_sc[h], v, preferred_element_type=F32) / sums[h]
        out = jnp.where(heads[h], o, out)
    o_ref[0] = out.astype(BF16)


def _band_bias(rel_bias):
    r = jnp.arange(BAND_TQ)[:, None]
    j = jnp.arange(BAND_TK)[None, :]
    qc = r // CHUNK
    kc = j // CHUNK
    in_band = (kc >= qc) & (kc <= qc + BAND_CHUNKS)
    n = BAND_TQ + BAND_TK
    m = jnp.arange(n)
    delta = jnp.where(m < BAND_TK, m, m - n)
    by_delta = rel_bias.astype(F32)[:, jnp.clip(2 * BAND_TQ - delta, -REL_CLIP, REL_CLIP) + REL_CLIP]
    flat = jnp.tile(by_delta, (1, BAND_TQ))[:, :BAND_TQ * (n - 1)]
    bias = flat.reshape(HB, BAND_TQ, n - 1)[:, :, :BAND_TK]
    return jnp.where(in_band[None], bias * LOG2E, NEG_INF)


def _band_attn(hm3, bias):
    b, s, _ = hm3.shape
    w = HB * DB
    cq, ck, cv = 6, 7, 8

    def kv_spec(col, back):
        return pl.BlockSpec((1, BAND_TQ, w), lambda bi, i: (bi, jnp.maximum(i - back, 0), col))

    return pl.pallas_call(
        _band_attn_body,
        grid=(b, s // BAND_TQ),
        in_specs=[pl.BlockSpec((1, BAND_TQ, w), lambda bi, i: (bi, i, cq)),
                  kv_spec(ck, 2), kv_spec(ck, 1), kv_spec(ck, 0),
                  kv_spec(cv, 2), kv_spec(cv, 1), kv_spec(cv, 0),
                  pl.BlockSpec((HB, BAND_TQ, BAND_TK), lambda bi, i: (0, 0, 0))],
        out_specs=pl.BlockSpec((1, BAND_TQ, w), lambda bi, i: (bi, i, 0)),
        out_shape=jax.ShapeDtypeStruct((b, s, w), BF16),
        scratch_shapes=[pltpu.VMEM((HB, BAND_TQ, BAND_TK), F32), pltpu.VMEM((HB, BAND_TQ, BAND_TK), BF16)],
        compiler_params=_params(("parallel", "parallel")),
        name="band_attn",
    )(hm3, hm3, hm3, hm3, hm3, hm3, hm3, bias)


I16_MIN = -2 ** 15
NEG_INF_KEY = -1900671691
UNSELECTED = -1e33
DSA_T = 256


def _order_key(x):
    bits = lax.bitcast_convert_type(x, I32)
    return jnp.where(bits < 0, bits ^ jnp.int32(0x7FFFFFFF), bits)


def _dsa_body(q_ref, k_ref, v_ref, qi_ref, ki_ref, w_ref, o_ref, key_sc, half_sc, base_sc, z_sc, z2_sc,
              p_sc, *state, n_sel, slopes):
    t = DSA_T
    qi = pl.program_id(1)
    q0 = qi * t
    nblk = qi + 1
    kk = lax.broadcasted_iota(I32, (t, t), 0)
    tg = q0 + lax.broadcasted_iota(I32, (t, t), 1)

    def block_start(j):
        return pl.multiple_of(j * t, t)

    qidx = qi_ref[0]
    lane_i = lax.broadcasted_iota(I32, (1, HI * DI), 1)
    zero_i = jnp.zeros_like(qidx)
    qm = [jnp.where((lane_i >> 5) == h, qidx, zero_i) for h in range(HI)]
    wt = w_ref[0].T

    def score_step(j, carry):
        k0 = block_start(j)
        kib = ki_ref[0, pl.ds(k0, t), :]
        sc = jnp.zeros((t, t), F32)
        for h in range(HI):
            rel = jnp.maximum(lax.dot_general(kib, qm[h], _NT, preferred_element_type=F32), 0.0)
            sc = sc + wt[h:h + 1, :] * rel
        allowed = ((k0 + kk) >> CHUNK_SHIFT) <= (tg >> CHUNK_SHIFT)
        key = _order_key(jnp.where(allowed, sc, NEG_INF))
        key_sc[pl.ds(k0, t), :] = key
        half_sc[pl.ds(k0, t), :] = (key >> 16).astype(I16)
        return carry

    lax.fori_loop(0, nblk, score_step, 0)

    def pipeline(produce, consume):
        produce(0, z_sc)

        def pair_step(i, carry):
            produce(2 * i + 1, z2_sc)
            consume(2 * i, z_sc)

            @pl.when(2 * i + 1 <= qi)
            def _():
                produce(2 * i + 2, z_sc)
                consume(2 * i + 1, z2_sc)
            return carry

        lax.fori_loop(0, (qi + 2) // 2, pair_step, 0)

    kf = float(n_sel)

    def count_half(cmp):
        def blk(j, acc):
            hit = jnp.where(cmp(half_sc[pl.ds(block_start(j), t), :]), jnp.int16(1), jnp.int16(0))
            for g in range(t // 32):
                acc = acc + hit[g * 32:(g + 1) * 32, :]
            return acc
        acc = lax.fori_loop(0, nblk, blk, jnp.zeros((32, t), I16))
        return jnp.sum(acc.astype(F32), axis=0, keepdims=True)

    def search_half(need):
        def count_ge(cand):
            c16 = cand.astype(I16)
            return count_half(lambda x: x >= c16)

        best = jnp.where(count_ge(jnp.zeros((1, t), I32)) >= need, 0, I16_MIN).astype(I32)

        def bit_step(i, best):
            cand = best + jnp.left_shift(jnp.int32(1), 14 - i)
            return jnp.where(count_ge(cand) >= need, cand, best)

        return lax.fori_loop(0, 15, bit_step, best)

    upper = search_half(kf)
    upper16 = upper.astype(I16)
    n_above = count_half(lambda x: x > upper16)

    def lower_step(j, carry):
        k0 = block_start(j)
        key = key_sc[pl.ds(k0, t), :]
        low = (key & 0xFFFF) + I16_MIN
        half_sc[pl.ds(k0, t), :] = jnp.where((key >> 16) == upper, low, I16_MIN).astype(I16)
        return carry

    lax.fori_loop(0, nblk, lower_step, 0)
    lower = search_half(kf - n_above)
    thr = upper * 65536 + (lower - I16_MIN)

    def alibi_or_unselected(keep, k0):
        kg = k0 + kk
        allowed = (kg >> CHUNK_SHIFT) <= (tg >> CHUNK_SHIFT)
        return jnp.where(keep, jnp.where(allowed, -jnp.abs(tg - kg).astype(F32), UNSELECTED), UNSELECTED)

    def select_step(j, n_ge):
        k0 = block_start(j)
        ge = key_sc[pl.ds(k0, t), :] >= thr
        base_sc[pl.ds(k0, t), :] = alibi_or_unselected(ge, k0)
        return n_ge + _col_sum32(jnp.where(ge, 1.0, 0.0))

    n_ge = jnp.sum(lax.fori_loop(0, nblk, select_step, jnp.zeros((32, t), F32)), axis=0, keepdims=True)
    over = jnp.where(n_ge > kf, jnp.where(thr != NEG_INF_KEY, 1.0, 0.0), 0.0)

    @pl.when(jnp.max(over) > 0.0)
    def _():
        def gt_step(j, acc):
            kb = key_sc[pl.ds(block_start(j), t), :]
            return acc + _col_sum32(jnp.where(kb > thr, 1.0, 0.0))
        n_gt = jnp.sum(lax.fori_loop(0, nblk, gt_step, jnp.zeros((32, t), F32)), axis=0, keepdims=True)
        need = kf - n_gt
        earlier = jnp.where(lax.broadcasted_iota(I32, (t, t), 1) < kk, 1.0, 0.0).astype(BF16)

        def tie_step(j, offs):
            k0 = block_start(j)
            kb = key_sc[pl.ds(k0, t), :]
            e = jnp.where(kb == thr, 1.0, 0.0)
            rank = jnp.dot(earlier, e.astype(BF16), preferred_element_type=F32) + offs
            keep = jnp.where(kb > thr, 1.0, jnp.where(rank < need, e, 0.0))
            base_sc[pl.ds(k0, t), :] = alibi_or_unselected(keep > 0.0, k0)
            return offs + jnp.sum(e, axis=0, keepdims=True)

        lax.fori_loop(0, nblk, tie_step, jnp.zeros((1, t), F32))

    q = q_ref[0]
    lane = lax.broadcasted_iota(I32, (1, HC * DC), 1)
    zero = jnp.zeros_like(q)
    q_head = [jnp.where((lane >> 6) == h, q, zero) for h in range(HC)]
    m_sc, l_sc, acc_sc = state[:HC], state[HC:2 * HC], state[2 * HC:]
    for h in range(HC):
        m_sc[h][...] = jnp.full((1, t), NEG_INF, F32)
        l_sc[h][...] = jnp.zeros((1, t), F32)
        acc_sc[h][...] = jnp.zeros((DC, t), F32)

    def scores(j, z_dst):
        k0 = block_start(jnp.minimum(j, qi))
        kb = k_ref[0, pl.ds(k0, t), :]
        base = base_sc[pl.ds(k0, t), :]
        for h in range(HC):
            z_dst[h] = lax.dot_general(kb, q_head[h], _NT, preferred_element_type=F32) + slopes[h] * base

    def accumulate(j, z_src):
        k0 = block_start(j)
        alphas = []
        for h in range(HC):
            z = z_src[h]
            m_old = m_sc[h][...]
            m_new = jnp.maximum(m_old, jnp.max(z, axis=0, keepdims=True))
            alpha = jnp.exp2(m_old - m_new)
            p = jnp.exp2(z - m_new)
            l_sc[h][...] = alpha * l_sc[h][...] + jnp.sum(p, axis=0, keepdims=True)
            m_sc[h][...] = m_new
            p_sc[h] = p.astype(BF16)
            alphas.append(alpha)
        vt = v_ref[0, pl.ds(k0, t), :].T
        for h in range(HC):
            acc_sc[h][...] = alphas[h] * acc_sc[h][...] + jnp.dot(
                vt[h * DC:(h + 1) * DC, :], p_sc[h], preferred_element_type=F32)

    pipeline(scores, accumulate)
    heads = [acc_sc[h][...] / l_sc[h][...] for h in range(HC)]
    o_ref[0] = jnp.concatenate(heads, axis=0).T.astype(BF16)


def _dsa_attn(hm3, hik3, hw3, slopes):
    b, s, _ = hm3.shape
    t = DSA_T
    w = HC * DC
    cq, ck, cv = 9, 10, 11
    body = functools.partial(_dsa_body, n_sel=min(TOPK_MAX, s // 4), slopes=slopes)
    return pl.pallas_call(
        body,
        grid=(b, s // t),
        in_specs=[pl.BlockSpec((1, t, w), lambda bi, i: (bi, i, cq)),
                  pl.BlockSpec((1, s, w), lambda bi, i: (bi, 0, ck)),
                  pl.BlockSpec((1, s, w), lambda bi, i: (bi, 0, cv)),
                  pl.BlockSpec((1, t, HI * DI), lambda bi, i: (bi, i, 0)),
                  pl.BlockSpec((1, s, HI * DI), lambda bi, i: (bi, 0, 1)),
                  pl.BlockSpec((1, t, D_W), lambda bi, i: (bi, i, 0))],
        out_specs=pl.BlockSpec((1, t, w), lambda bi, i: (bi, i, 0)),
        out_shape=jax.ShapeDtypeStruct((b, s, w), BF16),
        scratch_shapes=([pltpu.VMEM((s, t), I32), pltpu.VMEM((s, t), I16), pltpu.VMEM((s, t), F32),
                         pltpu.VMEM((HC, t, t), F32), pltpu.VMEM((HC, t, t), F32),
                         pltpu.VMEM((HC, t, t), BF16)]
                        + [pltpu.VMEM((1, t), F32)] * (2 * HC) + [pltpu.VMEM((DC, t), F32)] * HC),
        compiler_params=_params(("parallel", "arbitrary")),
        name="dsa_attn",
    )(hm3, hm3, hm3, hik3, hik3, hw3)


def _layer_norm(y, g, b):
    mu = jnp.mean(y, axis=-1, keepdims=True)
    d = y - mu
    var = jnp.mean(d * d, axis=-1, keepdims=True)
    return d * lax.rsqrt(var + EPS) * g + b


def _out_router_body(a_ref, b_ref, c_ref, x_ref, wo_ref, g_ref, beta_ref, wr_ref, br_ref,
                     x1_ref, meta_ref, cnt_ref, *, tm):
    i = pl.program_id(0)
    wa = HA * 2 * DA
    wb = wa + HB * DB
    y = (jnp.dot(a_ref[...], wo_ref[0:wa, :], preferred_element_type=F32)
         + jnp.dot(b_ref[...], wo_ref[wa:wb, :], preferred_element_type=F32)
         + jnp.dot(c_ref[...], wo_ref[wb:, :], preferred_element_type=F32))
    x1 = _layer_norm(DEEPNORM_ALPHA * x_ref[...] + y, g_ref[...], beta_ref[...])
    x1_ref[...] = x1

    x_hi = x1.astype(BF16)
    x_lo = (x1 - x_hi.astype(F32)).astype(BF16)
    wr = wr_ref[...]
    w_hi = wr.astype(BF16)
    w_lo = (wr - w_hi.astype(F32)).astype(BF16)
    by_hi = lax.dot_general(jnp.concatenate([w_hi, w_lo], axis=0), x_hi, _NT, preferred_element_type=F32)
    by_lo = lax.dot_general(w_hi, x_lo, _NT, preferred_element_type=F32)
    logits = by_hi[:N_EXPERTS] + by_hi[N_EXPERTS:] + by_lo + br_ref[...]
    row_e = lax.broadcasted_iota(I32, (N_EXPERTS, tm), 0).astype(F32)
    sel = jnp.zeros((N_EXPERTS, tm), F32)
    work = logits
    picks, vals = [], []
    for _ in range(TOP_K):
        v = jnp.max(work, axis=0, keepdims=True)
        e = jnp.min(jnp.where(work == v, row_e, float(N_EXPERTS)), axis=0, keepdims=True)
        hit = row_e == e
        sel = jnp.where(hit, 1.0, sel)
        work = jnp.where(hit, -jnp.inf, work)
        picks.append(e)
        vals.append(v)
    ex = [jnp.exp(v - vals[0]) for v in vals]
    den = ex[0] + ex[1] + ex[2] + ex[3]

    @pl.when(i == 0)
    def _():
        cnt_ref[...] = jnp.zeros_like(cnt_ref)

    earlier = jnp.where(lax.broadcasted_iota(I32, (tm, tm), 0) < lax.broadcasted_iota(I32, (tm, tm), 1),
                        1.0, 0.0).astype(BF16)
    so_far = cnt_ref[...]
    before = jnp.dot(sel.astype(BF16), earlier, preferred_element_type=F32) + so_far[:, 0:1]
    cnt_ref[...] = so_far + jnp.sum(sel, axis=1, keepdims=True)

    rows = lax.broadcasted_iota(I32, (4 * TOP_K, tm), 0)
    meta_t = jnp.zeros((4 * TOP_K, tm), F32)
    for kk in range(TOP_K):
        rank = jnp.sum(jnp.where(row_e == picks[kk], before, 0.0), axis=0, keepdims=True)
        meta_t = jnp.where(rows == kk, picks[kk], meta_t)
        meta_t = jnp.where(rows == TOP_K + kk, rank, meta_t)
        meta_t = jnp.where(rows == 2 * TOP_K + kk, ex[kk] / den, meta_t)
    meta_ref[...] = jnp.concatenate([meta_t, jnp.zeros((128 - 4 * TOP_K, tm), F32)], axis=0).T


def _out_router(oa, ob, oc, x2d, wo, g, beta, wr, br):
    t = x2d.shape[0]
    tm = 512
    row = lambda i: (i, 0)
    fix = lambda i: (0, 0)
    body = functools.partial(_out_router_body, tm=tm)
    return pl.pallas_call(
        body,
        grid=(t // tm,),
        in_specs=[pl.BlockSpec((tm, oa.shape[1]), row), pl.BlockSpec((tm, ob.shape[1]), row),
                  pl.BlockSpec((tm, oc.shape[1]), row), pl.BlockSpec((tm, D_MODEL), row),
                  pl.BlockSpec((D_MIX, D_MODEL), fix), pl.BlockSpec((1, D_MODEL), fix),
                  pl.BlockSpec((1, D_MODEL), fix), pl.BlockSpec((N_EXPERTS, D_MODEL), fix),
                  pl.BlockSpec((N_EXPERTS, 1), fix)],
        out_specs=[pl.BlockSpec((tm, D_MODEL), row),
                   pl.BlockSpec((tm, 128), row), pl.BlockSpec((N_EXPERTS, 128), fix)],
        out_shape=[jax.ShapeDtypeStruct((t, D_MODEL), F32),
                   jax.ShapeDtypeStruct((t, 128), F32), jax.ShapeDtypeStruct((N_EXPERTS, 128), F32)],
        compiler_params=_params(("arbitrary",)),
        name="out_router",
    )(oa, ob, oc, x2d, wo, g, beta, wr, br)


MOE_ROWS = 512
DISPATCH_TOKENS = 512


def _dispatch_body(dest_ref, pad_ref, x_ref, rows_out, zero_sc, sem, zero_sem):
    @pl.when(pl.program_id(0) == 0)
    def _():
        zero_sc[...] = jnp.zeros_like(zero_sc)
        n_blocks = pad_ref.shape[0]

        def zero_copy(b):
            start = pl.multiple_of(b * MOE_ROWS, MOE_ROWS)
            return pltpu.make_async_copy(zero_sc, rows_out.at[pl.ds(start, MOE_ROWS), :], zero_sem)

        def zero_start(b, carry):
            @pl.when(pad_ref[b] > 0)
            def _():
                zero_copy(b).start()
            return carry

        def zero_wait(b, carry):
            @pl.when(pad_ref[b] > 0)
            def _():
                zero_copy(b).wait()
            return carry

        lax.fori_loop(0, n_blocks, zero_start, 0)
        lax.fori_loop(0, n_blocks, zero_wait, 0)

    def token_pair(r2, carry):
        for u in range(2):
            r = 2 * r2 + u
            for kk in range(TOP_K):
                d = dest_ref[r * TOP_K + kk]
                pltpu.make_async_copy(x_ref.at[pl.ds(r, 1), :], rows_out.at[pl.ds(d, 1), :],
                                      sem).start(priority=kk % 2)
        return carry

    lax.fori_loop(0, DISPATCH_TOKENS // 2, token_pair, 0)
    for _ in range(TOP_K):
        pltpu.make_async_copy(x_ref, x_ref, sem).wait()


def _dispatch(dest_flat, block_padded, x1, n_rows):
    t = x1.shape[0]
    tm = DISPATCH_TOKENS
    return pl.pallas_call(
        _dispatch_body,
        grid=(t // tm,),
        in_specs=[pl.BlockSpec((tm * TOP_K,), lambda i: (i,), memory_space=pltpu.SMEM),
                  pl.BlockSpec(memory_space=pltpu.SMEM),
                  pl.BlockSpec((tm, D_MODEL), lambda i: (i, 0))],
        out_specs=pl.BlockSpec(memory_space=pl.ANY),
        out_shape=jax.ShapeDtypeStruct((n_rows, D_MODEL), F32),
        scratch_shapes=[pltpu.VMEM((MOE_ROWS, D_MODEL), F32), pltpu.SemaphoreType.DMA(()),
                        pltpu.SemaphoreType.DMA(())],
        compiler_params=_params(("arbitrary",)),
        name="moe_dispatch",
    )(dest_flat, block_padded, x1)


def _moe_body(be_ref, bv_ref, src_ref, x_ref, wgu_ref, bgu_ref, wd_ref, bd_ref, y_ref, wgu_sc, wd_sc):
    del src_ref
    i = pl.program_id(0)
    prev = be_ref[jnp.maximum(i - 1, 0)]

    @pl.when((i == 0) | (be_ref[i] != prev))
    def _():
        wgu_sc[...] = wgu_ref[0, 0].astype(BF16)
        wd_sc[...] = wd_ref[0, 0].astype(BF16)

    @pl.when(bv_ref[i] > 0)
    def _():
        h = jnp.dot(x_ref[...].astype(BF16), wgu_sc[...], preferred_element_type=F32) + bgu_ref[0, 0]
        gate = jnp.minimum(h[:, :D_EXPERT], SWIGLU_LIMIT)
        up = jnp.clip(h[:, D_EXPERT:], -SWIGLU_LIMIT, SWIGLU_LIMIT)
        glu = gate * (1.0 / (1.0 + jnp.exp(-SWIGLU_ALPHA * gate)))
        act = ((up + 1.0) * glu).astype(BF16)
        y_ref[...] = jnp.dot(act, wd_sc[...], preferred_element_type=F32) + bd_ref[0, 0]

    @pl.when(bv_ref[i] == 0)
    def _():
        y_ref[...] = jnp.zeros_like(y_ref)


def _moe(layer, block_expert, block_valid, block_src, x_rows, wgu, bgu, wd, bd):
    n_rows = x_rows.shape[0]
    expert = lambda i, be, bv, src: (layer, be[i], 0, 0)
    grid_spec = pltpu.PrefetchScalarGridSpec(
        num_scalar_prefetch=3,
        grid=(n_rows // MOE_ROWS,),
        in_specs=[pl.BlockSpec((MOE_ROWS, D_MODEL), lambda i, be, bv, src: (src[i], 0)),
                  pl.BlockSpec((1, 1, D_MODEL, 2 * D_EXPERT), expert),
                  pl.BlockSpec((1, 1, 1, 2 * D_EXPERT), expert),
                  pl.BlockSpec((1, 1, D_EXPERT, D_MODEL), expert),
                  pl.BlockSpec((1, 1, 1, D_MODEL), expert)],
        out_specs=pl.BlockSpec((MOE_ROWS, D_MODEL), lambda i, be, bv, src: (i, 0)),
        scratch_shapes=[pltpu.VMEM((D_MODEL, 2 * D_EXPERT), BF16), pltpu.VMEM((D_EXPERT, D_MODEL), BF16)],
    )
    return pl.pallas_call(
        _moe_body,
        grid_spec=grid_spec,
        out_shape=jax.ShapeDtypeStruct((n_rows, D_MODEL), F32),
        compiler_params=_params(("arbitrary",)),
        name="moe_experts",
    )(block_expert, block_valid, block_src, x_rows, wgu, bgu, wd, bd)


def _combine_body(x_ref, y0_ref, y1_ref, y2_ref, y3_ref, meta_ref, g_ref, beta_ref, o_ref):
    gates = meta_ref[...]
    ffn = (gates[:, 8:9] * y0_ref[0] + gates[:, 9:10] * y1_ref[0]
           + gates[:, 10:11] * y2_ref[0] + gates[:, 11:12] * y3_ref[0])
    o_ref[...] = _layer_norm(DEEPNORM_ALPHA * x_ref[...] + ffn, g_ref[...], beta_ref[...])


def _combine(x1, ys, meta, g, beta):
    t = x1.shape[0]
    tm = 512
    row = lambda i: (i, 0)
    fix = lambda i: (0, 0)
    slot = lambda kk: pl.BlockSpec((1, tm, D_MODEL), lambda i: (kk, i, 0))
    return pl.pallas_call(
        _combine_body,
        grid=(t // tm,),
        in_specs=[pl.BlockSpec((tm, D_MODEL), row)] + [slot(kk) for kk in range(TOP_K)]
                 + [pl.BlockSpec((tm, 128), row), pl.BlockSpec((1, D_MODEL), fix),
                    pl.BlockSpec((1, D_MODEL), fix)],
        out_specs=pl.BlockSpec((tm, D_MODEL), row),
        out_shape=jax.ShapeDtypeStruct((t, D_MODEL), F32),
        compiler_params=_params(("parallel",)),
        name="combine_ln",
    )(x1, ys, ys, ys, ys, meta, g, beta)


def _prep_w_in(w):
    scale = DA ** -0.5 * LOG2E
    col = jnp.ones((D_QKV,), F32)
    for lo, width in ((0, HA * 2 * DA), (3 * HA * 2 * DA, HB * DB),
                      (3 * HA * 2 * DA + 3 * HB * DB, HC * DC)):
        col = col.at[lo:lo + width].set(scale)
    main = w[:, :D_QKV] * col
    q_idx = w[:, D_QKV:D_QKV + HI * DI]
    k_idx = w[:, D_QKV + HI * DI:D_QKV + HI * DI + DI]
    w_idx = w[:, D_QKV + HI * DI + DI:] * ((HI ** -0.5) * (DI ** -0.5))
    w_pad = jnp.zeros((D_MODEL, D_W - HI), F32)
    return jnp.concatenate([main, q_idx, jnp.tile(k_idx, (1, HI)), w_idx, w_pad], axis=1).astype(BF16)


def _route(meta, cnt, n_tok):
    top_e = meta[:, 0:TOP_K].astype(I32)
    rank = meta[:, TOP_K:2 * TOP_K].astype(I32)
    counts = cnt[:, 0].astype(I32)
    n_blocks = n_tok * TOP_K // MOE_ROWS + N_EXPERTS
    blocks_e = (counts + MOE_ROWS - 1) // MOE_ROWS
    blocks_end = jnp.cumsum(blocks_e)
    row_start = (blocks_end - blocks_e) * MOE_ROWS
    experts = jnp.arange(N_EXPERTS, dtype=I32)
    start_of = jnp.sum(jnp.where(top_e[:, :, None] == experts, row_start, 0), axis=-1)
    dest = start_of + rank
    blk = jnp.arange(n_blocks, dtype=I32)
    block_expert = jnp.minimum(jnp.sum((blocks_end[None, :] <= blk[:, None]).astype(I32), axis=1),
                               N_EXPERTS - 1)
    block_valid = (blk < blocks_end[-1]).astype(I32)
    block_src = jnp.minimum(blk, blocks_end[-1] - 1)
    is_last = jnp.any((blocks_end[None, :] == blk[:, None] + 1) & (blocks_e[None, :] > 0), axis=1)
    block_padded = jnp.where(block_valid > 0, is_last, True).astype(I32)
    return dest, block_padded, n_blocks * MOE_ROWS, block_expert, block_valid, block_src


def kernel(x, w_in, lam_q1, lam_k1, lam_q2, lam_k2, subln_g, rel_bias, w_out, ln1_g, ln1_b,
           w_router, b_router, w_gu, b_gu, w_down, b_down, ln2_g, ln2_b):
    bsz, seq, _ = x.shape
    n_tok = bsz * seq
    slopes = [2.0 ** (-8.0 * i / (HA + HC)) for i in range(1, HA + HC + 1)]
    slopes_a = tuple(s * LOG2E for s in slopes[0::2])
    slopes_c = tuple(s * LOG2E for s in slopes[1::2])
    x2d = x.reshape(n_tok, D_MODEL)
    for l in range(DEPTH):
        lam_init = 0.8 - 0.6 * math.exp(-0.3 * l)
        lam = (jnp.exp(jnp.sum(lam_q1[l] * lam_k1[l])) - jnp.exp(jnp.sum(lam_q2[l] * lam_k2[l]))
               + lam_init).reshape(1).astype(F32)
        hm, hik, hw = _in_proj(x2d, _prep_w_in(w_in[l]))
        hm3 = hm.reshape(bsz, seq, D_QKV)
        out_a = _diff_attn(hm3, slopes_a, lam, subln_g[l].reshape(1, 2 * DA), lam_init)
        out_b = _band_attn(hm3, _band_bias(rel_bias[l]))
        out_c = _dsa_attn(hm3, hik.reshape(bsz, seq, D_IK), hw.reshape(bsz, seq, D_W), slopes_c)
        x1, meta, cnt = _out_router(
            out_a.reshape(n_tok, -1), out_b.reshape(n_tok, -1), out_c.reshape(n_tok, -1), x2d,
            w_out[l].astype(BF16), ln1_g[l].reshape(1, -1), ln1_b[l].reshape(1, -1),
            w_router[l].T, b_router[l].reshape(-1, 1))
        dest, block_padded, n_rows, block_expert, block_valid, block_src = _route(meta, cnt, n_tok)
        x_rows = _dispatch(dest.reshape(-1), block_padded, x1, n_rows)
        y_rows = _moe(l, block_expert, block_valid, block_src, x_rows, w_gu,
                      b_gu.reshape(DEPTH, N_EXPERTS, 1, -1), w_down, b_down.reshape(DEPTH, N_EXPERTS, 1, -1))
        ys = y_rows[dest.T]
        x2d = _combine(x1, ys, meta, ln2_g[l].reshape(1, -1), ln2_b[l].reshape(1, -1))
    return x2d.reshape(bsz, seq, D_MODEL)
```

```python
import functools
import math

import jax
import jax.numpy as jnp
from jax import lax
from jax.experimental import pallas as pl
from jax.experimental.pallas import tpu as pltpu

F32 = jnp.float32
BF16 = jnp.bfloat16
I32 = jnp.int32
I16 = jnp.int16

D_MODEL = 1024
DEPTH = 4
CHUNK = 64
CHUNK_SHIFT = 6
HA, DA = 4, 64
HB, DB = 4, 64
BAND_CHUNKS = 8
REL_CLIP = 128
HC, DC = 4, 64
HI, DI = 8, 32
TOPK_MAX = 256
N_EXPERTS = 32
TOP_K = 4
D_EXPERT = D_MODEL
SWIGLU_LIMIT = 7.0
SWIGLU_ALPHA = 1.702
DEEPNORM_ALPHA = (2 * DEPTH) ** 0.25
EPS = 1e-5
NEG_INF = -1e30
LOG2E = 1.4426950408889634

D_QKV = 3 * HA * 2 * DA + 3 * HB * DB + 3 * HC * DC
D_IK = 2 * HI * DI
D_W = 128
D_MIX = HA * 2 * DA + HB * DB + HC * DC

VMEM_LIMIT = 56 * 1024 * 1024

_NT = (((1,), (1,)), ((), ()))
_TN = (((0,), (0,)), ((), ()))


def _params(sem):
    return pltpu.CompilerParams(dimension_semantics=sem, vmem_limit_bytes=VMEM_LIMIT)


def _pack_bf16_pairs(y):
    half = y.shape[1] // 2
    lo = lax.bitcast_convert_type(y[:, :half].astype(BF16).astype(F32), I32)
    hi = lax.bitcast_convert_type(y[:, half:].astype(BF16).astype(F32), I32)
    return hi | lax.shift_right_logical(lo, jnp.int32(16))


def _unpack_bf16_pairs(u):
    lo = lax.bitcast_convert_type(lax.shift_left(u, jnp.int32(16)), F32)
    hi = lax.bitcast_convert_type(u & jnp.int32(-65536), F32)
    return jnp.concatenate([lo, hi], axis=1)


def _col_sum32(x):
    n, t = x.shape
    return x.reshape(n // 32, 32, t).sum(axis=0)


def _in_proj_body(x_ref, w_ref, hm_ref, hik_ref, hw_ref):
    xb = x_ref[...].astype(BF16)
    for n0 in range(0, D_QKV, 512):
        hm_ref[:, n0:n0 + 512] = jnp.dot(
            xb, w_ref[:, n0:n0 + 512], preferred_element_type=F32).astype(BF16)
    hik_ref[...] = jnp.dot(xb, w_ref[:, D_QKV:D_QKV + D_IK], preferred_element_type=F32).astype(BF16)
    hw_ref[...] = jnp.dot(xb, w_ref[:, D_QKV + D_IK:], preferred_element_type=F32)


def _in_proj(x2d, w):
    t = x2d.shape[0]
    tm = 512
    dw = D_QKV + D_IK + D_W
    return pl.pallas_call(
        _in_proj_body,
        grid=(t // tm,),
        in_specs=[pl.BlockSpec((tm, D_MODEL), lambda i: (i, 0)),
                  pl.BlockSpec((D_MODEL, dw), lambda i: (0, 0))],
        out_specs=[pl.BlockSpec((tm, D_QKV), lambda i: (i, 0)),
                   pl.BlockSpec((tm, D_IK), lambda i: (i, 0)),
                   pl.BlockSpec((tm, D_W), lambda i: (i, 0))],
        out_shape=[jax.ShapeDtypeStruct((t, D_QKV), BF16),
                   jax.ShapeDtypeStruct((t, D_IK), BF16),
                   jax.ShapeDtypeStruct((t, D_W), F32)],
        compiler_params=_params(("parallel",)),
        name="in_proj",
    )(x2d, w)


def _diff_attn_body(lam_ref, q_ref, k_ref, v_ref, g_ref, o_ref,
                    bias_sc, z_sc, z2_sc, p_sc, *state, t, slopes, out_scale):
    qi = pl.program_id(1)
    lam = lam_ref[0]
    dh = 2 * DA
    lane = lax.broadcasted_iota(I32, (1, dh), 1)
    q_half = []
    for h in range(HA):
        qh = q_ref[0, :, h * dh:(h + 1) * dh]
        zero = jnp.zeros_like(qh)
        q_half.append((jnp.where(lane < DA, qh, zero), jnp.where(lane >= DA, qh, zero)))

    @pl.when(qi == 0)
    def _():
        kk = lax.broadcasted_iota(I32, (t, t), 0)
        tt = lax.broadcasted_iota(I32, (t, t), 1)
        own_chunk = (kk >> CHUNK_SHIFT) <= (tt >> CHUNK_SHIFT)
        for h in range(HA):
            bias_sc[h] = slopes[h] * kk.astype(F32)
            bias_sc[HA + h] = jnp.where(
                own_chunk, slopes[h] * jnp.minimum(kk, 2 * tt - kk).astype(F32), NEG_INF)

    n_chain = 2 * HA
    m_sc, l_sc, acc_sc = state[:n_chain], state[n_chain:2 * n_chain], state[2 * n_chain:]
    for c in range(n_chain):
        m_sc[c][...] = jnp.full((1, t), NEG_INF, F32)
        l_sc[c][...] = jnp.zeros((1, t), F32)
        acc_sc[c][...] = jnp.zeros((dh, t), F32)

    def scores(j, z_dst):
        jb = jnp.minimum(j, qi)
        k0 = pl.multiple_of(jb * t, t)
        bias_base = jnp.where(jb == qi, HA, 0)
        for h in range(HA):
            kb = k_ref[0, pl.ds(k0, t), h * dh:(h + 1) * dh]
            bias = bias_sc[bias_base + h]
            for a in range(2):
                z_dst[2 * h + a] = lax.dot_general(
                    kb, q_half[h][a], _NT, preferred_element_type=F32) + bias

    def accumulate(j, z_src):
        k0 = pl.multiple_of(j * t, t)
        k0f = k0.astype(F32)
        alphas = []
        for c in range(n_chain):
            shift = slopes[c // 2] * k0f
            z = z_src[c]
            m_old = m_sc[c][...]
            m_new = jnp.maximum(m_old, jnp.max(z, axis=0, keepdims=True) + shift)
            alpha = jnp.exp2(m_old - m_new)
            p = jnp.exp2(z - (m_new - shift))
            l_sc[c][...] = alpha * l_sc[c][...] + jnp.sum(p, axis=0, keepdims=True)
            m_sc[c][...] = m_new
            p_sc[c] = p.astype(BF16)
            alphas.append(alpha)
        for h in range(HA):
            vt = v_ref[0, pl.ds(k0, t), h * dh:(h + 1) * dh].T
            for a in range(2):
                c = 2 * h + a
                acc_sc[c][...] = alphas[c] * acc_sc[c][...] + jnp.dot(
                    vt, p_sc[c], preferred_element_type=F32)

    scores(0, z_sc)

    def pair_step(i, carry):
        scores(2 * i + 1, z2_sc)
        accumulate(2 * i, z_sc)

        @pl.when(2 * i + 1 <= qi)
        def _():
            scores(2 * i + 2, z_sc)
            accumulate(2 * i + 1, z2_sc)
        return carry

    lax.fori_loop(0, (qi + 2) // 2, pair_step, 0)

    outs = []
    for h in range(HA):
        ot = (acc_sc[2 * h][...] / l_sc[2 * h][...]
              - lam * (acc_sc[2 * h + 1][...] / l_sc[2 * h + 1][...]))
        ms = jnp.mean(ot * ot, axis=0, keepdims=True)
        outs.append((ot * lax.rsqrt(ms + EPS)).T * g_ref[...] * out_scale)
    o_ref[0] = jnp.concatenate(outs, axis=1).astype(BF16)


def _diff_attn(hm3, slopes, lam, sub_g, lam_init):
    b, s, _ = hm3.shape
    t = 256
    dh = 2 * DA
    w = HA * dh
    body = functools.partial(_diff_attn_body, t=t, slopes=slopes, out_scale=1.0 - lam_init)
    return pl.pallas_call(
        body,
        grid=(b, s // t),
        in_specs=[pl.BlockSpec(memory_space=pltpu.SMEM),
                  pl.BlockSpec((1, t, w), lambda bi, qi: (bi, qi, 0)),
                  pl.BlockSpec((1, s, w), lambda bi, qi: (bi, 0, 1)),
                  pl.BlockSpec((1, s, w), lambda bi, qi: (bi, 0, 2)),
                  pl.BlockSpec((1, dh), lambda bi, qi: (0, 0))],
        out_specs=pl.BlockSpec((1, t, w), lambda bi, qi: (bi, qi, 0)),
        out_shape=jax.ShapeDtypeStruct((b, s, w), BF16),
        scratch_shapes=([pltpu.VMEM((2 * HA, t, t), F32), pltpu.VMEM((2 * HA, t, t), F32),
                         pltpu.VMEM((2 * HA, t, t), F32), pltpu.VMEM((2 * HA, t, t), BF16)]
                        + [pltpu.VMEM((1, t), F32)] * (4 * HA) + [pltpu.VMEM((dh, t), F32)] * (2 * HA)),
        compiler_params=_params(("parallel", "arbitrary")),
        name="diff_attn",
    )(lam, hm3, hm3, hm3, sub_g)


BAND_TQ = 256
BAND_TK = 3 * BAND_TQ


def _band_attn_body(q_ref, k0_ref, k1_ref, k2_ref, v0_ref, v1_ref, v2_ref, bias_ref, o_ref, s_sc, p_sc):
    i = pl.program_id(1)
    q = q_ref[0]
    k = jnp.concatenate([k0_ref[0], k1_ref[0], k2_ref[0]], axis=0)
    v = jnp.concatenate([v0_ref[0], v1_ref[0], v2_ref[0]], axis=0)
    col = lax.broadcasted_iota(I32, (1, BAND_TK), 1)
    first_valid = jnp.where(i >= 2, 0, jnp.where(i >= 1, BAND_TQ, 2 * BAND_TQ))
    kvalid = col >= first_valid
    lane = lax.broadcasted_iota(I32, (1, HB * DB), 1)
    zero = jnp.zeros_like(q)
    heads = [(lane >> 6) == h for h in range(HB)]
    for h in range(HB):
        s = lax.dot_general(jnp.where(heads[h], q, zero), k, _NT, preferred_element_type=F32)
        s_sc[h] = jnp.where(kvalid, s + bias_ref[h], NEG_INF)
    sums = []
    for h in range(HB):
        s = s_sc[h]
        p = jnp.exp2(s - jnp.max(s, axis=-1, keepdims=True))
        sums.append(jnp.sum(p, axis=-1, keepdims=True))
        p_sc[h] = p.astype(BF16)
    out = jnp.zeros((BAND_TQ, HB * DB), F32)
    for h in range(HB):
        o = jnp.dot(p_sc[h], v, preferred_element_type=F32) / sums[h]
        out = jnp.where(heads[h], o, out)
    o_ref[0] = out.astype(BF16)


def _band_bias(rel_bias):
    r = jnp.arange(BAND_TQ)[:, None]
    j = jnp.arange(BAND_TK)[None, :]
    qc = r // CHUNK
    kc = j // CHUNK
    in_band = (kc >= qc) & (kc <= qc + BAND_CHUNKS)
    n = BAND_TQ + BAND_TK
    m = jnp.arange(n)
    delta = jnp.where(m < BAND_TK, m, m - n)
    by_delta = rel_bias.astype(F32)[:, jnp.clip(2 * BAND_TQ - delta, -REL_CLIP, REL_CLIP) + REL_CLIP]
    flat = jnp.tile(by_delta, (1, BAND_TQ))[:, :BAND_TQ * (n - 1)]
    bias = flat.reshape(HB, BAND_TQ, n - 1)[:, :, :BAND_TK]
    return jnp.where(in_band[None], bias * LOG2E, NEG_INF)


def _band_attn(hm3, bias):
    b, s, _ = hm3.shape
    w = HB * DB
    cq, ck, cv = 6, 7, 8

    def kv_spec(col, back):
        return pl.BlockSpec((1, BAND_TQ, w), lambda bi, i: (bi, jnp.maximum(i - back, 0), col))

    return pl.pallas_call(
        _band_attn_body,
        grid=(b, s // BAND_TQ),
        in_specs=[pl.BlockSpec((1, BAND_TQ, w), lambda bi, i: (bi, i, cq)),
                  kv_spec(ck, 2), kv_spec(ck, 1), kv_spec(ck, 0),
                  kv_spec(cv, 2), kv_spec(cv, 1), kv_spec(cv, 0),
                  pl.BlockSpec((HB, BAND_TQ, BAND_TK), lambda bi, i: (0, 0, 0))],
        out_specs=pl.BlockSpec((1, BAND_TQ, w), lambda bi, i: (bi, i, 0)),
        out_shape=jax.ShapeDtypeStruct((b, s, w), BF16),
        scratch_shapes=[pltpu.VMEM((HB, BAND_TQ, BAND_TK), F32), pltpu.VMEM((HB, BAND_TQ, BAND_TK), BF16)],
        compiler_params=_params(("parallel", "parallel")),
        name="band_attn",
    )(hm3, hm3, hm3, hm3, hm3, hm3, hm3, bias)


I16_MIN = -2 ** 15
NEG_INF_KEY = -1900671691
UNSELECTED = -1e33
DSA_T = 256


def _order_key(x):
    bits = lax.bitcast_convert_type(x, I32)
    return jnp.where(bits < 0, bits ^ jnp.int32(0x7FFFFFFF), bits)


def _dsa_body(q_ref, k_ref, v_ref, qi_ref, ki_ref, w_ref, o_ref, key_sc, half_sc, base_sc, z_sc, z2_sc,
              p_sc, *state, n_sel, slopes):
    t = DSA_T
    qi = pl.program_id(1)
    q0 = qi * t
    nblk = qi + 1
    kk = lax.broadcasted_iota(I32, (t, t), 0)
    tg = q0 + lax.broadcasted_iota(I32, (t, t), 1)

    def block_start(j):
        return pl.multiple_of(j * t, t)

    qidx = qi_ref[0]
    lane_i = lax.broadcasted_iota(I32, (1, HI * DI), 1)
    zero_i = jnp.zeros_like(qidx)
    qm = [jnp.where((lane_i >> 5) == h, qidx, zero_i) for h in range(HI)]
    wt = w_ref[0].T

    def score_step(j, carry):
        k0 = block_start(j)
        kib = ki_ref[0, pl.ds(k0, t), :]
        sc = jnp.zeros((t, t), F32)
        for h in range(HI):
            rel = jnp.maximum(lax.dot_general(kib, qm[h], _NT, preferred_element_type=F32), 0.0)
            sc = sc + wt[h:h + 1, :] * rel
        allowed = ((k0 + kk) >> CHUNK_SHIFT) <= (tg >> CHUNK_SHIFT)
        key = _order_key(jnp.where(allowed, sc, NEG_INF))
        key_sc[pl.ds(k0, t), :] = key
        half_sc[pl.ds(k0, t), :] = (key >> 16).astype(I16)
        return carry

    lax.fori_loop(0, nblk, score_step, 0)

    def pipeline(produce, consume):
        produce(0, z_sc)

        def pair_step(i, carry):
            produce(2 * i + 1, z2_sc)
            consume(2 * i, z_sc)

            @pl.when(2 * i + 1 <= qi)
            def _():
                produce(2 * i + 2, z_sc)
                consume(2 * i + 1, z2_sc)
            return carry

        lax.fori_loop(0, (qi + 2) // 2, pair_step, 0)

    kf = float(n_sel)

    def count_half(cmp):
        def blk(j, acc):
            hit = jnp.where(cmp(half_sc[pl.ds(block_start(j), t), :]), jnp.int16(1), jnp.int16(0))
            for g in range(t // 32):
                acc = acc + hit[g * 32:(g + 1) * 32, :]
            return acc
        acc = lax.fori_loop(0, nblk, blk, jnp.zeros((32, t), I16))
        return jnp.sum(acc.astype(F32), axis=0, keepdims=True)

    def search_half(need):
        def count_ge(cand):
            c16 = cand.astype(I16)
            return count_half(lambda x: x >= c16)

        best = jnp.where(count_ge(jnp.zeros((1, t), I32)) >= need, 0, I16_MIN).astype(I32)

        def bit_step(i, best):
            cand = best + jnp.left_shift(jnp.int32(1), 14 - i)
            return jnp.where(count_ge(cand) >= need, cand, best)

        return lax.fori_loop(0, 15, bit_step, best)

    upper = search_half(kf)
    upper16 = upper.astype(I16)
    n_above = count_half(lambda x: x > upper16)

    def lower_step(j, carry):
        k0 = block_start(j)
        key = key_sc[pl.ds(k0, t), :]
        low = (key & 0xFFFF) + I16_MIN
        half_sc[pl.ds(k0, t), :] = jnp.where((key >> 16) == upper, low, I16_MIN).astype(I16)
        return carry

    lax.fori_loop(0, nblk, lower_step, 0)
    lower = search_half(kf - n_above)
    thr = upper * 65536 + (lower - I16_MIN)

    def alibi_or_unselected(keep, k0):
        kg = k0 + kk
        allowed = (kg >> CHUNK_SHIFT) <= (tg >> CHUNK_SHIFT)
        return jnp.where(keep, jnp.where(allowed, -jnp.abs(tg - kg).astype(F32), UNSELECTED), UNSELECTED)

    def select_step(j, n_ge):
        k0 = block_start(j)
        ge = key_sc[pl.ds(k0, t), :] >= thr
        base_sc[pl.ds(k0, t), :] = alibi_or_unselected(ge, k0)
        return n_ge + _col_sum32(jnp.where(ge, 1.0, 0.0))

    n_ge = jnp.sum(lax.fori_loop(0, nblk, select_step, jnp.zeros((32, t), F32)), axis=0, keepdims=True)
    over = jnp.where(n_ge > kf, jnp.where(thr != NEG_INF_KEY, 1.0, 0.0), 0.0)

    @pl.when(jnp.max(over) > 0.0)
    def _():
        def gt_step(j, acc):
            kb = key_sc[pl.ds(block_start(j), t), :]
            return acc + _col_sum32(jnp.where(kb > thr, 1.0, 0.0))
        n_gt = jnp.sum(lax.fori_loop(0, nblk, gt_step, jnp.zeros((32, t), F32)), axis=0, keepdims=True)
        need = kf - n_gt
        earlier = jnp.where(lax.broadcasted_iota(I32, (t, t), 1) < kk, 1.0, 0.0).astype(BF16)

        def tie_step(j, offs):
            k0 = block_start(j)
            kb = key_sc[pl.ds(k0, t), :]
            e = jnp.where(kb == thr, 1.0, 0.0)
            rank = jnp.dot(earlier, e.astype(BF16), preferred_element_type=F32) + offs
            keep = jnp.where(kb > thr, 1.0, jnp.where(rank < need, e, 0.0))
            base_sc[pl.ds(k0, t), :] = alibi_or_unselected(keep > 0.0, k0)
            return offs + jnp.sum(e, axis=0, keepdims=True)

        lax.fori_loop(0, nblk, tie_step, jnp.zeros((1, t), F32))

    q = q_ref[0]
    lane = lax.broadcasted_iota(I32, (1, HC * DC), 1)
    zero = jnp.zeros_like(q)
    q_head = [jnp.where((lane >> 6) == h, q, zero) for h in range(HC)]
    m_sc, l_sc, acc_sc = state[:HC], state[HC:2 * HC], state[2 * HC:]
    for h in range(HC):
        m_sc[h][...] = jnp.full((1, t), NEG_INF, F32)
        l_sc[h][...] = jnp.zeros((1, t), F32)
        acc_sc[h][...] = jnp.zeros((DC, t), F32)

    def scores(j, z_dst):
        k0 = block_start(jnp.minimum(j, qi))
        kb = k_ref[0, pl.ds(k0, t), :]
        base = base_sc[pl.ds(k0, t), :]
        for h in range(HC):
            z_dst[h] = lax.dot_general(kb, q_head[h], _NT, preferred_element_type=F32) + slopes[h] * base

    def accumulate(j, z_src):
        k0 = block_start(j)
        alphas = []
        for h in range(HC):
            z = z_src[h]
            m_old = m_sc[h][...]
            m_new = jnp.maximum(m_old, jnp.max(z, axis=0, keepdims=True))
            alpha = jnp.exp2(m_old - m_new)
            p = jnp.exp2(z - m_new)
            l_sc[h][...] = alpha * l_sc[h][...] + jnp.sum(p, axis=0, keepdims=True)
            m_sc[h][...] = m_new
            p_sc[h] = p.astype(BF16)
            alphas.append(alpha)
        vt = v_ref[0, pl.ds(k0, t), :].T
        for h in range(HC):
            acc_sc[h][...] = alphas[h] * acc_sc[h][...] + jnp.dot(
                vt[h * DC:(h + 1) * DC, :], p_sc[h], preferred_element_type=F32)

    pipeline(scores, accumulate)
    heads = [acc_sc[h][...] / l_sc[h][...] for h in range(HC)]
    o_ref[0] = jnp.concatenate(heads, axis=0).T.astype(BF16)


def _dsa_attn(hm3, hik3, hw3, slopes):
    b, s, _ = hm3.shape
    t = DSA_T
    w = HC * DC
    cq, ck, cv = 9, 10, 11
    body = functools.partial(_dsa_body, n_sel=min(TOPK_MAX, s // 4), slopes=slopes)
    return pl.pallas_call(
        body,
        grid=(b, s // t),
        in_specs=[pl.BlockSpec((1, t, w), lambda bi, i: (bi, i, cq)),
                  pl.BlockSpec((1, s, w), lambda bi, i: (bi, 0, ck)),
                  pl.BlockSpec((1, s, w), lambda bi, i: (bi, 0, cv)),
                  pl.BlockSpec((1, t, HI * DI), lambda bi, i: (bi, i, 0)),
                  pl.BlockSpec((1, s, HI * DI), lambda bi, i: (bi, 0, 1)),
                  pl.BlockSpec((1, t, D_W), lambda bi, i: (bi, i, 0))],
        out_specs=pl.BlockSpec((1, t, w), lambda bi, i: (bi, i, 0)),
        out_shape=jax.ShapeDtypeStruct((b, s, w), BF16),
        scratch_shapes=([pltpu.VMEM((s, t), I32), pltpu.VMEM((s, t), I16), pltpu.VMEM((s, t), F32),
                         pltpu.VMEM((HC, t, t), F32), pltpu.VMEM((HC, t, t), F32),
                         pltpu.VMEM((HC, t, t), BF16)]
                        + [pltpu.VMEM((1, t), F32)] * (2 * HC) + [pltpu.VMEM((DC, t), F32)] * HC),
        compiler_params=_params(("parallel", "arbitrary")),
        name="dsa_attn",
    )(hm3, hm3, hm3, hik3, hik3, hw3)


def _layer_norm(y, g, b):
    mu = jnp.mean(y, axis=-1, keepdims=True)
    d = y - mu
    var = jnp.mean(d * d, axis=-1, keepdims=True)
    return d * lax.rsqrt(var + EPS) * g + b


def _out_router_body(a_ref, b_ref, c_ref, x_ref, wo_ref, g_ref, beta_ref, wr_ref, br_ref,
                     x1_ref, meta_ref, cnt_ref, *, tm):
    i = pl.program_id(0)
    wa = HA * 2 * DA
    wb = wa + HB * DB
    y = (jnp.dot(a_ref[...], wo_ref[0:wa, :], preferred_element_type=F32)
         + jnp.dot(b_ref[...], wo_ref[wa:wb, :], preferred_element_type=F32)
         + jnp.dot(c_ref[...], wo_ref[wb:, :], preferred_element_type=F32))
    x1 = _layer_norm(DEEPNORM_ALPHA * x_ref[...] + y, g_ref[...], beta_ref[...])
    x1_ref[...] = x1

    x_hi = x1.astype(BF16)
    x_lo = (x1 - x_hi.astype(F32)).astype(BF16)
    wr = wr_ref[...]
    w_hi = wr.astype(BF16)
    w_lo = (wr - w_hi.astype(F32)).astype(BF16)
    by_hi = lax.dot_general(jnp.concatenate([w_hi, w_lo], axis=0), x_hi, _NT, preferred_element_type=F32)
    by_lo = lax.dot_general(w_hi, x_lo, _NT, preferred_element_type=F32)
    logits = by_hi[:N_EXPERTS] + by_hi[N_EXPERTS:] + by_lo + br_ref[...]
    row_e = lax.broadcasted_iota(I32, (N_EXPERTS, tm), 0).astype(F32)
    sel = jnp.zeros((N_EXPERTS, tm), F32)
    work = logits
    picks, vals = [], []
    for _ in range(TOP_K):
        v = jnp.max(work, axis=0, keepdims=True)
        e = jnp.min(jnp.where(work == v, row_e, float(N_EXPERTS)), axis=0, keepdims=True)
        hit = row_e == e
        sel = jnp.where(hit, 1.0, sel)
        work = jnp.where(hit, -jnp.inf, work)
        picks.append(e)
        vals.append(v)
    ex = [jnp.exp(v - vals[0]) for v in vals]
    den = ex[0] + ex[1] + ex[2] + ex[3]

    @pl.when(i == 0)
    def _():
        cnt_ref[...] = jnp.zeros_like(cnt_ref)

    earlier = jnp.where(lax.broadcasted_iota(I32, (tm, tm), 0) < lax.broadcasted_iota(I32, (tm, tm), 1),
                        1.0, 0.0).astype(BF16)
    so_far = cnt_ref[...]
    before = jnp.dot(sel.astype(BF16), earlier, preferred_element_type=F32) + so_far[:, 0:1]
    cnt_ref[...] = so_far + jnp.sum(sel, axis=1, keepdims=True)

    rows = lax.broadcasted_iota(I32, (4 * TOP_K, tm), 0)
    meta_t = jnp.zeros((4 * TOP_K, tm), F32)
    for kk in range(TOP_K):
        rank = jnp.sum(jnp.where(row_e == picks[kk], before, 0.0), axis=0, keepdims=True)
        meta_t = jnp.where(rows == kk, picks[kk], meta_t)
        meta_t = jnp.where(rows == TOP_K + kk, rank, meta_t)
        meta_t = jnp.where(rows == 2 * TOP_K + kk, ex[kk] / den, meta_t)
    meta_ref[...] = jnp.concatenate([meta_t, jnp.zeros((128 - 4 * TOP_K, tm), F32)], axis=0).T


def _out_router(oa, ob, oc, x2d, wo, g, beta, wr, br):
    t = x2d.shape[0]
    tm = 512
    row = lambda i: (i, 0)
    fix = lambda i: (0, 0)
    body = functools.partial(_out_router_body, tm=tm)
    return pl.pallas_call(
        body,
        grid=(t // tm,),
        in_specs=[pl.BlockSpec((tm, oa.shape[1]), row), pl.BlockSpec((tm, ob.shape[1]), row),
                  pl.BlockSpec((tm, oc.shape[1]), row), pl.BlockSpec((tm, D_MODEL), row),
                  pl.BlockSpec((D_MIX, D_MODEL), fix), pl.BlockSpec((1, D_MODEL), fix),
                  pl.BlockSpec((1, D_MODEL), fix), pl.BlockSpec((N_EXPERTS, D_MODEL), fix),
                  pl.BlockSpec((N_EXPERTS, 1), fix)],
        out_specs=[pl.BlockSpec((tm, D_MODEL), row),
                   pl.BlockSpec((tm, 128), row), pl.BlockSpec((N_EXPERTS, 128), fix)],
        out_shape=[jax.ShapeDtypeStruct((t, D_MODEL), F32),
                   jax.ShapeDtypeStruct((t, 128), F32), jax.ShapeDtypeStruct((N_EXPERTS, 128), F32)],
        compiler_params=_params(("arbitrary",)),
        name="out_router",
    )(oa, ob, oc, x2d, wo, g, beta, wr, br)


MOE_ROWS = 512
DISPATCH_TOKENS = 512


def _dispatch_body(dest_ref, pad_ref, x_ref, rows_out, zero_sc, sem, zero_sem):
    @pl.when(pl.program_id(0) == 0)
    def _():
        zero_sc[...] = jnp.zeros_like(zero_sc)
        n_blocks = pad_ref.shape[0]

        def zero_copy(b):
            start = pl.multiple_of(b * MOE_ROWS, MOE_ROWS)
            return pltpu.make_async_copy(zero_sc, rows_out.at[pl.ds(start, MOE_ROWS), :], zero_sem)

        def zero_start(b, carry):
            @pl.when(pad_ref[b] > 0)
            def _():
                zero_copy(b).start()
            return carry

        def zero_wait(b, carry):
            @pl.when(pad_ref[b] > 0)
            def _():
                zero_copy(b).wait()
            return carry

        lax.fori_loop(0, n_blocks, zero_start, 0)
        lax.fori_loop(0, n_blocks, zero_wait, 0)

    def token_pair(r2, carry):
        for u in range(2):
            r = 2 * r2 + u
            for kk in range(TOP_K):
                d = dest_ref[r * TOP_K + kk]
                pltpu.make_async_copy(x_ref.at[pl.ds(r, 1), :], rows_out.at[pl.ds(d, 1), :],
                                      sem).start(priority=kk % 2)
        return carry

    lax.fori_loop(0, DISPATCH_TOKENS // 2, token_pair, 0)
    for _ in range(TOP_K):
        pltpu.make_async_copy(x_ref, x_ref, sem).wait()


def _dispatch(dest_flat, block_padded, x1, n_rows):
    t = x1.shape[0]
    tm = DISPATCH_TOKENS
    return pl.pallas_call(
        _dispatch_body,
        grid=(t // tm,),
        in_specs=[pl.BlockSpec((tm * TOP_K,), lambda i: (i,), memory_space=pltpu.SMEM),
                  pl.BlockSpec(memory_space=pltpu.SMEM),
                  pl.BlockSpec((tm, D_MODEL), lambda i: (i, 0))],
        out_specs=pl.BlockSpec(memory_space=pl.ANY),
        out_shape=jax.ShapeDtypeStruct((n_rows, D_MODEL), F32),
        scratch_shapes=[pltpu.VMEM((MOE_ROWS, D_MODEL), F32), pltpu.SemaphoreType.DMA(()),
                        pltpu.SemaphoreType.DMA(())],
        compiler_params=_params(("arbitrary",)),
        name="moe_dispatch",
    )(dest_flat, block_padded, x1)


def _moe_body(be_ref, bv_ref, src_ref, x_ref, wgu_ref, bgu_ref, wd_ref, bd_ref, y_ref, wgu_sc, wd_sc):
    del src_ref
    i = pl.program_id(0)
    prev = be_ref[jnp.maximum(i - 1, 0)]

    @pl.when((i == 0) | (be_ref[i] != prev))
    def _():
        wgu_sc[...] = wgu_ref[0, 0].astype(BF16)
        wd_sc[...] = wd_ref[0, 0].astype(BF16)

    @pl.when(bv_ref[i] > 0)
    def _():
        h = jnp.dot(x_ref[...].astype(BF16), wgu_sc[...], preferred_element_type=F32) + bgu_ref[0, 0]
        gate = jnp.minimum(h[:, :D_EXPERT], SWIGLU_LIMIT)
        up = jnp.clip(h[:, D_EXPERT:], -SWIGLU_LIMIT, SWIGLU_LIMIT)
        glu = gate * (1.0 / (1.0 + jnp.exp(-SWIGLU_ALPHA * gate)))
        act = ((up + 1.0) * glu).astype(BF16)
        y = jnp.dot(act, wd_sc[...], preferred_element_type=F32) + bd_ref[0, 0]
        y_ref[...] = _pack_bf16_pairs(y)

    @pl.when(bv_ref[i] == 0)
    def _():
        y_ref[...] = jnp.zeros_like(y_ref)


def _moe(layer, block_expert, block_valid, block_src, x_rows, wgu, bgu, wd, bd):
    n_rows = x_rows.shape[0]
    expert = lambda i, be, bv, src: (layer, be[i], 0, 0)
    grid_spec = pltpu.PrefetchScalarGridSpec(
        num_scalar_prefetch=3,
        grid=(n_rows // MOE_ROWS,),
        in_specs=[pl.BlockSpec((MOE_ROWS, D_MODEL), lambda i, be, bv, src: (src[i], 0)),
                  pl.BlockSpec((1, 1, D_MODEL, 2 * D_EXPERT), expert),
                  pl.BlockSpec((1, 1, 1, 2 * D_EXPERT), expert),
                  pl.BlockSpec((1, 1, D_EXPERT, D_MODEL), expert),
                  pl.BlockSpec((1, 1, 1, D_MODEL), expert)],
        out_specs=pl.BlockSpec((MOE_ROWS, D_MODEL // 2), lambda i, be, bv, src: (i, 0)),
        scratch_shapes=[pltpu.VMEM((D_MODEL, 2 * D_EXPERT), BF16), pltpu.VMEM((D_EXPERT, D_MODEL), BF16)],
    )
    return pl.pallas_call(
        _moe_body,
        grid_spec=grid_spec,
        out_shape=jax.ShapeDtypeStruct((n_rows, D_MODEL // 2), I32),
        compiler_params=_params(("arbitrary",)),
        name="moe_experts",
    )(block_expert, block_valid, block_src, x_rows, wgu, bgu, wd, bd)


def _combine_body(x_ref, y0_ref, y1_ref, y2_ref, y3_ref, meta_ref, g_ref, beta_ref, o_ref):
    gates = meta_ref[...]
    ffn = (gates[:, 8:9] * _unpack_bf16_pairs(y0_ref[0]) + gates[:, 9:10] * _unpack_bf16_pairs(y1_ref[0])
           + gates[:, 10:11] * _unpack_bf16_pairs(y2_ref[0]) + gates[:, 11:12] * _unpack_bf16_pairs(y3_ref[0]))
    o_ref[...] = _layer_norm(DEEPNORM_ALPHA * x_ref[...] + ffn, g_ref[...], beta_ref[...])


def _combine(x1, ys, meta, g, beta):
    t = x1.shape[0]
    tm = 512
    row = lambda i: (i, 0)
    fix = lambda i: (0, 0)
    slot = lambda kk: pl.BlockSpec((1, tm, D_MODEL // 2), lambda i: (kk, i, 0))
    return pl.pallas_call(
        _combine_body,
        grid=(t // tm,),
        in_specs=[pl.BlockSpec((tm, D_MODEL), row)] + [slot(kk) for kk in range(TOP_K)]
                 + [pl.BlockSpec((tm, 128), row), pl.BlockSpec((1, D_MODEL), fix),
                    pl.BlockSpec((1, D_MODEL), fix)],
        out_specs=pl.BlockSpec((tm, D_MODEL), row),
        out_shape=jax.ShapeDtypeStruct((t, D_MODEL), F32),
        compiler_params=_params(("parallel",)),
        name="combine_ln",
    )(x1, ys, ys, ys, ys, meta, g, beta)


def _prep_w_in(w):
    scale = DA ** -0.5 * LOG2E
    col = jnp.ones((D_QKV,), F32)
    for lo, width in ((0, HA * 2 * DA), (3 * HA * 2 * DA, HB * DB),
                      (3 * HA * 2 * DA + 3 * HB * DB, HC * DC)):
        col = col.at[lo:lo + width].set(scale)
    main = w[:, :D_QKV] * col
    q_idx = w[:, D_QKV:D_QKV + HI * DI]
    k_idx = w[:, D_QKV + HI * DI:D_QKV + HI * DI + DI]
    w_idx = w[:, D_QKV + HI * DI + DI:] * ((HI ** -0.5) * (DI ** -0.5))
    w_pad = jnp.zeros((D_MODEL, D_W - HI), F32)
    return jnp.concatenate([main, q_idx, jnp.tile(k_idx, (1, HI)), w_idx, w_pad], axis=1).astype(BF16)


def _route(meta, cnt, n_tok):
    top_e = meta[:, 0:TOP_K].astype(I32)
    rank = meta[:, TOP_K:2 * TOP_K].astype(I32)
    counts = cnt[:, 0].astype(I32)
    n_blocks = n_tok * TOP_K // MOE_ROWS + N_EXPERTS
    blocks_e = (counts + MOE_ROWS - 1) // MOE_ROWS
    blocks_end = jnp.cumsum(blocks_e)
    row_start = (blocks_end - blocks_e) * MOE_ROWS
    experts = jnp.arange(N_EXPERTS, dtype=I32)
    start_of = jnp.sum(jnp.where(top_e[:, :, None] == experts, row_start, 0), axis=-1)
    dest = start_of + rank
    blk = jnp.arange(n_blocks, dtype=I32)
    block_expert = jnp.minimum(jnp.sum((blocks_end[None, :] <= blk[:, None]).astype(I32), axis=1),
                               N_EXPERTS - 1)
    block_valid = (blk < blocks_end[-1]).astype(I32)
    block_src = jnp.minimum(blk, blocks_end[-1] - 1)
    is_last = jnp.any((blocks_end[None, :] == blk[:, None] + 1) & (blocks_e[None, :] > 0), axis=1)
    block_padded = jnp.where(block_valid > 0, is_last, True).astype(I32)
    return dest, block_padded, n_blocks * MOE_ROWS, block_expert, block_valid, block_src


def kernel(x, w_in, lam_q1, lam_k1, lam_q2, lam_k2, subln_g, rel_bias, w_out, ln1_g, ln1_b,
           w_router, b_router, w_gu, b_gu, w_down, b_down, ln2_g, ln2_b):
    bsz, seq, _ = x.shape
    n_tok = bsz * seq
    slopes = [2.0 ** (-8.0 * i / (HA + HC)) for i in range(1, HA + HC + 1)]
    slopes_a = tuple(s * LOG2E for s in slopes[0::2])
    slopes_c = tuple(s * LOG2E for s in slopes[1::2])
    x2d = x.reshape(n_tok, D_MODEL)
    for l in range(DEPTH):
        lam_init = 0.8 - 0.6 * math.exp(-0.3 * l)
        lam = (jnp.exp(jnp.sum(lam_q1[l] * lam_k1[l])) - jnp.exp(jnp.sum(lam_q2[l] * lam_k2[l]))
               + lam_init).reshape(1).astype(F32)
        hm, hik, hw = _in_proj(x2d, _prep_w_in(w_in[l]))
        hm3 = hm.reshape(bsz, seq, D_QKV)
        out_a = _diff_attn(hm3, slopes_a, lam, subln_g[l].reshape(1, 2 * DA), lam_init)
        out_b = _band_attn(hm3, _band_bias(rel_bias[l]))
        out_c = _dsa_attn(hm3, hik.reshape(bsz, seq, D_IK), hw.reshape(bsz, seq, D_W), slopes_c)
        x1, meta, cnt = _out_router(
            out_a.reshape(n_tok, -1), out_b.reshape(n_tok, -1), out_c.reshape(n_tok, -1), x2d,
            w_out[l].astype(BF16), ln1_g[l].reshape(1, -1), ln1_b[l].reshape(1, -1),
            w_router[l].T, b_router[l].reshape(-1, 1))
        dest, block_padded, n_rows, block_expert, block_valid, block_src = _route(meta, cnt, n_tok)
        x_rows = _dispatch(dest.reshape(-1), block_padded, x1, n_rows)
        y_rows = _moe(l, block_expert, block_valid, block_src, x_rows, w_gu,
                      b_gu.reshape(DEPTH, N_EXPERTS, 1, -1), w_down, b_down.reshape(DEPTH, N_EXPERTS, 1, -1))
        ys = y_rows[dest.T]
        x2d = _combine(x1, ys, meta, ln2_g[l].reshape(1, -1), ln2_b[l].reshape(1, -1))
    return x2d.reshape(bsz, seq, D_MODEL)
```

```python
import functools
import math

import jax
import jax.numpy as jnp
from jax import lax
from jax.experimental import pallas as pl
from jax.experimental.pallas import tpu as pltpu

F32 = jnp.float32
BF16 = jnp.bfloat16
I32 = jnp.int32
I16 = jnp.int16

D_MODEL = 1024
DEPTH = 4
CHUNK = 64
CHUNK_SHIFT = 6
HA, DA = 4, 64
HB, DB = 4, 64
BAND_CHUNKS = 8
REL_CLIP = 128
HC, DC = 4, 64
HI, DI = 8, 32
TOPK_MAX = 256
N_EXPERTS = 32
TOP_K = 4
D_EXPERT = D_MODEL
SWIGLU_LIMIT = 7.0
SWIGLU_ALPHA = 1.702
DEEPNORM_ALPHA = (2 * DEPTH) ** 0.25
EPS = 1e-5
NEG_INF = -1e30
LOG2E = 1.4426950408889634

D_QKV = 3 * HA * 2 * DA + 3 * HB * DB + 3 * HC * DC
D_IK = 2 * HI * DI
D_W = 128
D_MIX = HA * 2 * DA + HB * DB + HC * DC

VMEM_LIMIT = 56 * 1024 * 1024

_NT = (((1,), (1,)), ((), ()))
_TN = (((0,), (0,)), ((), ()))


def _params(sem):
    return pltpu.CompilerParams(dimension_semantics=sem, vmem_limit_bytes=VMEM_LIMIT)


def _pack_bf16_pairs(y):
    half = y.shape[1] // 2
    lo = lax.bitcast_convert_type(y[:, :half].astype(BF16).astype(F32), I32)
    hi = lax.bitcast_convert_type(y[:, half:].astype(BF16).astype(F32), I32)
    return hi | lax.shift_right_logical(lo, jnp.int32(16))


def _unpack_bf16_pairs(u):
    lo = lax.bitcast_convert_type(lax.shift_left(u, jnp.int32(16)), F32)
    hi = lax.bitcast_convert_type(u & jnp.int32(-65536), F32)
    return jnp.concatenate([lo, hi], axis=1)


def _col_sum32(x):
    n, t = x.shape
    return x.reshape(n // 32, 32, t).sum(axis=0)


def _in_proj_body(x_ref, w_ref, hm_ref, hik_ref, hw_ref):
    xb = x_ref[...].astype(BF16)
    for n0 in range(0, D_QKV, 512):
        hm_ref[:, n0:n0 + 512] = jnp.dot(
            xb, w_ref[:, n0:n0 + 512], preferred_element_type=F32).astype(BF16)
    hik_ref[...] = jnp.dot(xb, w_ref[:, D_QKV:D_QKV + D_IK], preferred_element_type=F32).astype(BF16)
    hw_ref[...] = jnp.dot(xb, w_ref[:, D_QKV + D_IK:], preferred_element_type=F32)


def _in_proj(x2d, w):
    t = x2d.shape[0]
    tm = 512
    dw = D_QKV + D_IK + D_W
    return pl.pallas_call(
        _in_proj_body,
        grid=(t // tm,),
        in_specs=[pl.BlockSpec((tm, D_MODEL), lambda i: (i, 0)),
                  pl.BlockSpec((D_MODEL, dw), lambda i: (0, 0))],
        out_specs=[pl.BlockSpec((tm, D_QKV), lambda i: (i, 0)),
                   pl.BlockSpec((tm, D_IK), lambda i: (i, 0)),
                   pl.BlockSpec((tm, D_W), lambda i: (i, 0))],
        out_shape=[jax.ShapeDtypeStruct((t, D_QKV), BF16),
                   jax.ShapeDtypeStruct((t, D_IK), BF16),
                   jax.ShapeDtypeStruct((t, D_W), F32)],
        compiler_params=_params(("parallel",)),
        name="in_proj",
    )(x2d, w)


def _diff_attn_body(lam_ref, q_ref, k_ref, v_ref, g_ref, o_ref,
                    bias_sc, z_sc, z2_sc, p_sc, *state, t, slopes, out_scale):
    qi = pl.program_id(1)
    lam = lam_ref[0]
    dh = 2 * DA
    lane = lax.broadcasted_iota(I32, (1, dh), 1)
    q_half = []
    for h in range(HA):
        qh = q_ref[0, :, h * dh:(h + 1) * dh]
        zero = jnp.zeros_like(qh)
        q_half.append((jnp.where(lane < DA, qh, zero), jnp.where(lane >= DA, qh, zero)))

    @pl.when(qi == 0)
    def _():
        kk = lax.broadcasted_iota(I32, (t, t), 0)
        tt = lax.broadcasted_iota(I32, (t, t), 1)
        own_chunk = (kk >> CHUNK_SHIFT) <= (tt >> CHUNK_SHIFT)
        for h in range(HA):
            bias_sc[h] = slopes[h] * kk.astype(F32)
            bias_sc[HA + h] = jnp.where(
                own_chunk, slopes[h] * jnp.minimum(kk, 2 * tt - kk).astype(F32), NEG_INF)

    n_chain = 2 * HA
    m_sc, l_sc, acc_sc = state[:n_chain], state[n_chain:2 * n_chain], state[2 * n_chain:]
    for c in range(n_chain):
        m_sc[c][...] = jnp.full((1, t), NEG_INF, F32)
        l_sc[c][...] = jnp.zeros((1, t), F32)
        acc_sc[c][...] = jnp.zeros((dh, t), F32)

    def scores(j, z_dst):
        jb = jnp.minimum(j, qi)
        k0 = pl.multiple_of(jb * t, t)
        bias_base = jnp.where(jb == qi, HA, 0)
        for h in range(HA):
            kb = k_ref[0, pl.ds(k0, t), h * dh:(h + 1) * dh]
            bias = bias_sc[bias_base + h]
            for a in range(2):
                z_dst[2 * h + a] = lax.dot_general(
                    kb, q_half[h][a], _NT, preferred_element_type=F32) + bias

    def accumulate(j, z_src):
        k0 = pl.multiple_of(j * t, t)
        k0f = k0.astype(F32)
        alphas = []
        for c in range(n_chain):
            shift = slopes[c // 2] * k0f
            z = z_src[c]
            m_old = m_sc[c][...]
            m_new = jnp.maximum(m_old, jnp.max(z, axis=0, keepdims=True) + shift)
            alpha = jnp.exp2(m_old - m_new)
            p = jnp.exp2(z - (m_new - shift))
            l_sc[c][...] = alpha * l_sc[c][...] + jnp.sum(p, axis=0, keepdims=True)
            m_sc[c][...] = m_new
            p_sc[c] = p.astype(BF16)
            alphas.append(alpha)
        for h in range(HA):
            vt = v_ref[0, pl.ds(k0, t), h * dh:(h + 1) * dh].T
            for a in range(2):
                c = 2 * h + a
                acc_sc[c][...] = alphas[c] * acc_sc[c][...] + jnp.dot(
                    vt, p_sc[c], preferred_element_type=F32)

    scores(0, z_sc)

    def pair_step(i, carry):
        scores(2 * i + 1, z2_sc)
        accumulate(2 * i, z_sc)

        @pl.when(2 * i + 1 <= qi)
        def _():
            scores(2 * i + 2, z_sc)
            accumulate(2 * i + 1, z2_sc)
        return carry

    lax.fori_loop(0, (qi + 2) // 2, pair_step, 0)

    outs = []
    for h in range(HA):
        ot = (acc_sc[2 * h][...] / l_sc[2 * h][...]
              - lam * (acc_sc[2 * h + 1][...] / l_sc[2 * h + 1][...]))
        ms = jnp.mean(ot * ot, axis=0, keepdims=True)
        outs.append((ot * lax.rsqrt(ms + EPS)).T * g_ref[...] * out_scale)
    o_ref[0] = jnp.concatenate(outs, axis=1).astype(BF16)


def _diff_attn(hm3, slopes, lam, sub_g, lam_init):
    b, s, _ = hm3.shape
    t = 256
    dh = 2 * DA
    w = HA * dh
    body = functools.partial(_diff_attn_body, t=t, slopes=slopes, out_scale=1.0 - lam_init)
    return pl.pallas_call(
        body,
        grid=(b, s // t),
        in_specs=[pl.BlockSpec(memory_space=pltpu.SMEM),
                  pl.BlockSpec((1, t, w), lambda bi, qi: (bi, qi, 0)),
                  pl.BlockSpec((1, s, w), lambda bi, qi: (bi, 0, 1)),
                  pl.BlockSpec((1, s, w), lambda bi, qi: (bi, 0, 2)),
                  pl.BlockSpec((1, dh), lambda bi, qi: (0, 0))],
        out_specs=pl.BlockSpec((1, t, w), lambda bi, qi: (bi, qi, 0)),
        out_shape=jax.ShapeDtypeStruct((b, s, w), BF16),
        scratch_shapes=([pltpu.VMEM((2 * HA, t, t), F32), pltpu.VMEM((2 * HA, t, t), F32),
                         pltpu.VMEM((2 * HA, t, t), F32), pltpu.VMEM((2 * HA, t, t), BF16)]
                        + [pltpu.VMEM((1, t), F32)] * (4 * HA) + [pltpu.VMEM((dh, t), F32)] * (2 * HA)),
        compiler_params=_params(("parallel", "arbitrary")),
        name="diff_attn",
    )(lam, hm3, hm3, hm3, sub_g)


BAND_TQ = 256
BAND_TK = 3 * BAND_TQ


def _band_attn_body(q_ref, k0_ref, k1_ref, k2_ref, v0_ref, v1_ref, v2_ref, bias_ref, o_ref, s_sc, p_sc):
    i = pl.program_id(1)
    q = q_ref[0]
    k = jnp.concatenate([k0_ref[0], k1_ref[0], k2_ref[0]], axis=0)
    v = jnp.concatenate([v0_ref[0], v1_ref[0], v2_ref[0]], axis=0)
    col = lax.broadcasted_iota(I32, (1, BAND_TK), 1)
    first_valid = jnp.where(i >= 2, 0, jnp.where(i >= 1, BAND_TQ, 2 * BAND_TQ))
    kvalid = col >= first_valid
    lane = lax.broadcasted_iota(I32, (1, HB * DB), 1)
    zero = jnp.zeros_like(q)
    heads = [(lane >> 6) == h for h in range(HB)]
    for h in range(HB):
        s = lax.dot_general(jnp.where(heads[h], q, zero), k, _NT, preferred_element_type=F32)
        s_sc[h] = jnp.where(kvalid, s + bias_ref[h], NEG_INF)
    sums = []
    for h in range(HB):
        s = s_sc[h]
        p = jnp.exp2(s - jnp.max(s, axis=-1, keepdims=True))
        sums.append(jnp.sum(p, axis=-1, keepdims=True))
        p_sc[h] = p.astype(BF16)
    out = jnp.zeros((BAND_TQ, HB * DB), F32)
    for h in range(HB):
        o = jnp.dot(p_sc[h], v, preferred_element_type=F32) / sums[h]
        out = jnp.where(heads[h], o, out)
    o_ref[0] = out.astype(BF16)


def _band_bias(rel_bias):
    r = jnp.arange(BAND_TQ)[:, None]
    j = jnp.arange(BAND_TK)[None, :]
    qc = r // CHUNK
    kc = j // CHUNK
    in_band = (kc >= qc) & (kc <= qc + BAND_CHUNKS)
    n = BAND_TQ + BAND_TK
    m = jnp.arange(n)
    delta = jnp.where(m < BAND_TK, m, m - n)
    by_delta = rel_bias.astype(F32)[:, jnp.clip(2 * BAND_TQ - delta, -REL_CLIP, REL_CLIP) + REL_CLIP]
    flat = jnp.tile(by_delta, (1, BAND_TQ))[:, :BAND_TQ * (n - 1)]
    bias = flat.reshape(HB, BAND_TQ, n - 1)[:, :, :BAND_TK]
    return jnp.where(in_band[None], bias * LOG2E, NEG_INF)


def _band_attn(hm3, bias):
    b, s, _ = hm3.shape
    w = HB * DB
    cq, ck, cv = 6, 7, 8

    def kv_spec(col, back):
        return pl.BlockSpec((1, BAND_TQ, w), lambda bi, i: (bi, jnp.maximum(i - back, 0), col))

    return pl.pallas_call(
        _band_attn_body,
        grid=(b, s // BAND_TQ),
        in_specs=[pl.BlockSpec((1, BAND_TQ, w), lambda bi, i: (bi, i, cq)),
                  kv_spec(ck, 2), kv_spec(ck, 1), kv_spec(ck, 0),
                  kv_spec(cv, 2), kv_spec(cv, 1), kv_spec(cv, 0),
                  pl.BlockSpec((HB, BAND_TQ, BAND_TK), lambda bi, i: (0, 0, 0))],
        out_specs=pl.BlockSpec((1, BAND_TQ, w), lambda bi, i: (bi, i, 0)),
        out_shape=jax.ShapeDtypeStruct((b, s, w), BF16),
        scratch_shapes=[pltpu.VMEM((HB, BAND_TQ, BAND_TK), F32), pltpu.VMEM((HB, BAND_TQ, BAND_TK), BF16)],
        compiler_params=_params(("parallel", "parallel")),
        name="band_attn",
    )(hm3, hm3, hm3, hm3, hm3, hm3, hm3, bias)


I16_MIN = -2 ** 15
NEG_INF_KEY = -1900671691
UNSELECTED = -1e33
DSA_T = 256


def _order_key(x):
    bits = lax.bitcast_convert_type(x, I32)
    return jnp.where(bits < 0, bits ^ jnp.int32(0x7FFFFFFF), bits)


def _dsa_body(q_ref, k_ref, v_ref, qi_ref, ki_ref, w_ref, o_ref, key_sc, half_sc, base_sc, z_sc, z2_sc,
              p_sc, *state, n_sel, slopes):
    t = DSA_T
    qi = pl.program_id(1)
    q0 = qi * t
    nblk = qi + 1
    kk = lax.broadcasted_iota(I32, (t, t), 0)
    tg = q0 + lax.broadcasted_iota(I32, (t, t), 1)

    def block_start(j):
        return pl.multiple_of(j * t, t)

    qidx = qi_ref[0]
    lane_i = lax.broadcasted_iota(I32, (1, HI * DI), 1)
    zero_i = jnp.zeros_like(qidx)
    qm = [jnp.where((lane_i >> 5) == h, qidx, zero_i) for h in range(HI)]
    wt = w_ref[0].T

    def score_step(j, carry):
        k0 = block_start(j)
        kib = ki_ref[0, pl.ds(k0, t), :]
        sc = jnp.zeros((t, t), F32)
        for h in range(HI):
            rel = jnp.maximum(lax.dot_general(kib, qm[h], _NT, preferred_element_type=F32), 0.0)
            sc = sc + wt[h:h + 1, :] * rel
        allowed = ((k0 + kk) >> CHUNK_SHIFT) <= (tg >> CHUNK_SHIFT)
        key = _order_key(jnp.where(allowed, sc, NEG_INF))
        key_sc[pl.ds(k0, t), :] = key
        half_sc[pl.ds(k0, t), :] = (key >> 16).astype(I16)
        return carry

    lax.fori_loop(0, nblk, score_step, 0)

    def pipeline(produce, consume):
        produce(0, z_sc)

        def pair_step(i, carry):
            produce(2 * i + 1, z2_sc)
            consume(2 * i, z_sc)

            @pl.when(2 * i + 1 <= qi)
            def _():
                produce(2 * i + 2, z_sc)
                consume(2 * i + 1, z2_sc)
            return carry

        lax.fori_loop(0, (qi + 2) // 2, pair_step, 0)

    kf = float(n_sel)

    def count_half(cmp):
        def blk(j, acc):
            hit = jnp.where(cmp(half_sc[pl.ds(block_start(j), t), :]), jnp.int16(1), jnp.int16(0))
            for g in range(t // 32):
                acc = acc + hit[g * 32:(g + 1) * 32, :]
            return acc
        acc = lax.fori_loop(0, nblk, blk, jnp.zeros((32, t), I16))
        return jnp.sum(acc.astype(F32), axis=0, keepdims=True)

    def search_half(need):
        def count_ge(cand):
            c16 = cand.astype(I16)
            return count_half(lambda x: x >= c16)

        best = jnp.where(count_ge(jnp.zeros((1, t), I32)) >= need, 0, I16_MIN).astype(I32)

        def bit_step(i, best):
            cand = best + jnp.left_shift(jnp.int32(1), 14 - i)
            return jnp.where(count_ge(cand) >= need, cand, best)

        return lax.fori_loop(0, 15, bit_step, best)

    upper = search_half(kf)
    upper16 = upper.astype(I16)
    n_above = count_half(lambda x: x > upper16)

    def lower_step(j, carry):
        k0 = block_start(j)
        key = key_sc[pl.ds(k0, t), :]
        low = (key & 0xFFFF) + I16_MIN
        half_sc[pl.ds(k0, t), :] = jnp.where((key >> 16) == upper, low, I16_MIN).astype(I16)
        return carry

    lax.fori_loop(0, nblk, lower_step, 0)
    lower = search_half(kf - n_above)
    thr = upper * 65536 + (lower - I16_MIN)

    def alibi_or_unselected(keep, k0):
        kg = k0 + kk
        allowed = (kg >> CHUNK_SHIFT) <= (tg >> CHUNK_SHIFT)
        return jnp.where(keep, jnp.where(allowed, -jnp.abs(tg - kg).astype(F32), UNSELECTED), UNSELECTED)

    def select_step(j, n_ge):
        k0 = block_start(j)
        ge = key_sc[pl.ds(k0, t), :] >= thr
        base_sc[pl.ds(k0, t), :] = alibi_or_unselected(ge, k0)
        return n_ge + _col_sum32(jnp.where(ge, 1.0, 0.0))

    n_ge = jnp.sum(lax.fori_loop(0, nblk, select_step, jnp.zeros((32, t), F32)), axis=0, keepdims=True)
    over = jnp.where(n_ge > kf, jnp.where(thr != NEG_INF_KEY, 1.0, 0.0), 0.0)

    @pl.when(jnp.max(over) > 0.0)
    def _():
        def gt_step(j, acc):
            kb = key_sc[pl.ds(block_start(j), t), :]
            return acc + _col_sum32(jnp.where(kb > thr, 1.0, 0.0))
        n_gt = jnp.sum(lax.fori_loop(0, nblk, gt_step, jnp.zeros((32, t), F32)), axis=0, keepdims=True)
        need = kf - n_gt
        earlier = jnp.where(lax.broadcasted_iota(I32, (t, t), 1) < kk, 1.0, 0.0).astype(BF16)

        def tie_step(j, offs):
            k0 = block_start(j)
            kb = key_sc[pl.ds(k0, t), :]
            e = jnp.where(kb == thr, 1.0, 0.0)
            rank = jnp.dot(earlier, e.astype(BF16), preferred_element_type=F32) + offs
            keep = jnp.where(kb > thr, 1.0, jnp.where(rank < need, e, 0.0))
            base_sc[pl.ds(k0, t), :] = alibi_or_unselected(keep > 0.0, k0)
            return offs + jnp.sum(e, axis=0, keepdims=True)

        lax.fori_loop(0, nblk, tie_step, jnp.zeros((1, t), F32))

    q = q_ref[0]
    lane = lax.broadcasted_iota(I32, (1, HC * DC), 1)
    zero = jnp.zeros_like(q)
    q_head = [jnp.where((lane >> 6) == h, q, zero) for h in range(HC)]
    m_sc, l_sc, acc_sc = state[:HC], state[HC:2 * HC], state[2 * HC:]
    for h in range(HC):
        m_sc[h][...] = jnp.full((1, t), NEG_INF, F32)
        l_sc[h][...] = jnp.zeros((1, t), F32)
        acc_sc[h][...] = jnp.zeros((DC, t), F32)

    def scores(j, z_dst):
        k0 = block_start(jnp.minimum(j, qi))
        kb = k_ref[0, pl.ds(k0, t), :]
        base = base_sc[pl.ds(k0, t), :]
        for h in range(HC):
            z_dst[h] = lax.dot_general(kb, q_head[h], _NT, preferred_element_type=F32) + slopes[h] * base

    def accumulate(j, z_src):
        k0 = block_start(j)
        alphas = []
        for h in range(HC):
            z = z_src[h]
            m_old = m_sc[h][...]
            m_new = jnp.maximum(m_old, jnp.max(z, axis=0, keepdims=True))
            alpha = jnp.exp2(m_old - m_new)
            p = jnp.exp2(z - m_new)
            l_sc[h][...] = alpha * l_sc[h][...] + jnp.sum(p, axis=0, keepdims=True)
            m_sc[h][...] = m_new
            p_sc[h] = p.astype(BF16)
            alphas.append(alpha)
        vt = v_ref[0, pl.ds(k0, t), :].T
        for h in range(HC):
            acc_sc[h][...] = alphas[h] * acc_sc[h][...] + jnp.dot(
                vt[h * DC:(h + 1) * DC, :], p_sc[h], preferred_element_type=F32)

    pipeline(scores, accumulate)
    heads = [acc_sc[h][...] / l_sc[h][...] for h in range(HC)]
    o_ref[0] = jnp.concatenate(heads, axis=0).T.astype(BF16)


def _dsa_attn(hm3, hik3, hw3, slopes):
    b, s, _ = hm3.shape
    t = DSA_T
    w = HC * DC
    cq, ck, cv = 9, 10, 11
    body = functools.partial(_dsa_body, n_sel=min(TOPK_MAX, s // 4), slopes=slopes)
    return pl.pallas_call(
        body,
        grid=(b, s // t),
        in_specs=[pl.BlockSpec((1, t, w), lambda bi, i: (bi, i, cq)),
                  pl.BlockSpec((1, s, w), lambda bi, i: (bi, 0, ck)),
                  pl.BlockSpec((1, s, w), lambda bi, i: (bi, 0, cv)),
                  pl.BlockSpec((1, t, HI * DI), lambda bi, i: (bi, i, 0)),
                  pl.BlockSpec((1, s, HI * DI), lambda bi, i: (bi, 0, 1)),
                  pl.BlockSpec((1, t, D_W), lambda bi, i: (bi, i, 0))],
        out_specs=pl.BlockSpec((1, t, w), lambda bi, i: (bi, i, 0)),
        out_shape=jax.ShapeDtypeStruct((b, s, w), BF16),
        scratch_shapes=([pltpu.VMEM((s, t), I32), pltpu.VMEM((s, t), I16), pltpu.VMEM((s, t), F32),
                         pltpu.VMEM((HC, t, t), F32), pltpu.VMEM((HC, t, t), F32),
                         pltpu.VMEM((HC, t, t), BF16)]
                        + [pltpu.VMEM((1, t), F32)] * (2 * HC) + [pltpu.VMEM((DC, t), F32)] * HC),
        compiler_params=_params(("parallel", "arbitrary")),
        name="dsa_attn",
    )(hm3, hm3, hm3, hik3, hik3, hw3)


def _layer_norm(y, g, b):
    mu = jnp.mean(y, axis=-1, keepdims=True)
    d = y - mu
    var = jnp.mean(d * d, axis=-1, keepdims=True)
    return d * lax.rsqrt(var + EPS) * g + b


def _out_router_body(a_ref, b_ref, c_ref, x_ref, wo_ref, g_ref, beta_ref, wr_ref, br_ref,
                     x1_ref, meta_ref, slots_ref, cnt_ref, earlier_sc, *, tm):
    i = pl.program_id(0)
    wa = HA * 2 * DA
    wb = wa + HB * DB
    y = (jnp.dot(a_ref[...], wo_ref[0:wa, :], preferred_element_type=F32)
         + jnp.dot(b_ref[...], wo_ref[wa:wb, :], preferred_element_type=F32)
         + jnp.dot(c_ref[...], wo_ref[wb:, :], preferred_element_type=F32))
    x1 = _layer_norm(DEEPNORM_ALPHA * x_ref[...] + y, g_ref[...], beta_ref[...])
    x1_ref[...] = x1

    x_hi = x1.astype(BF16)
    x_lo = (x1 - x_hi.astype(F32)).astype(BF16)
    wr = wr_ref[...]
    w_hi = wr.astype(BF16)
    w_lo = (wr - w_hi.astype(F32)).astype(BF16)
    by_hi = lax.dot_general(jnp.concatenate([w_hi, w_lo], axis=0), x_hi, _NT, preferred_element_type=F32)
    by_lo = lax.dot_general(w_hi, x_lo, _NT, preferred_element_type=F32)
    logits = by_hi[:N_EXPERTS] + by_hi[N_EXPERTS:] + by_lo + br_ref[...]
    row_e = lax.broadcasted_iota(I32, (N_EXPERTS, tm), 0).astype(F32)
    sel = jnp.zeros((N_EXPERTS, tm), F32)
    work = logits
    picks, vals = [], []
    for _ in range(TOP_K):
        v = jnp.max(work, axis=0, keepdims=True)
        e = jnp.min(jnp.where(work == v, row_e, float(N_EXPERTS)), axis=0, keepdims=True)
        hit = row_e == e
        sel = jnp.where(hit, 1.0, sel)
        work = jnp.where(hit, -jnp.inf, work)
        picks.append(e)
        vals.append(v)
    ex = [jnp.exp(v - vals[0]) for v in vals]
    den = ex[0] + ex[1] + ex[2] + ex[3]

    @pl.when(i == 0)
    def _():
        cnt_ref[...] = jnp.zeros_like(cnt_ref)
        earlier_sc[...] = jnp.where(
            lax.broadcasted_iota(I32, (tm, tm), 0) < lax.broadcasted_iota(I32, (tm, tm), 1),
            1.0, 0.0).astype(BF16)

    so_far = cnt_ref[...]
    before = jnp.dot(sel.astype(BF16), earlier_sc[...], preferred_element_type=F32) + so_far[:, 0:1]
    cnt_ref[...] = so_far + jnp.sum(sel, axis=1, keepdims=True)

    rows = lax.broadcasted_iota(I32, (4 * TOP_K, tm), 0)
    meta_t = jnp.zeros((4 * TOP_K, tm), F32)
    for kk in range(TOP_K):
        rank = jnp.sum(jnp.where(row_e == picks[kk], before, 0.0), axis=0, keepdims=True)
        meta_t = jnp.where(rows == kk, picks[kk], meta_t)
        meta_t = jnp.where(rows == TOP_K + kk, rank, meta_t)
        meta_t = jnp.where(rows == 2 * TOP_K + kk, ex[kk] / den, meta_t)
    slots_ref[...] = meta_t
    meta_ref[...] = jnp.concatenate([meta_t, jnp.zeros((128 - 4 * TOP_K, tm), F32)], axis=0).T


def _out_router(oa, ob, oc, x2d, wo, g, beta, wr, br):
    t = x2d.shape[0]
    tm = 512
    row = lambda i: (i, 0)
    fix = lambda i: (0, 0)
    body = functools.partial(_out_router_body, tm=tm)
    return pl.pallas_call(
        body,
        grid=(t // tm,),
        in_specs=[pl.BlockSpec((tm, oa.shape[1]), row), pl.BlockSpec((tm, ob.shape[1]), row),
                  pl.BlockSpec((tm, oc.shape[1]), row), pl.BlockSpec((tm, D_MODEL), row),
                  pl.BlockSpec((D_MIX, D_MODEL), fix), pl.BlockSpec((1, D_MODEL), fix),
                  pl.BlockSpec((1, D_MODEL), fix), pl.BlockSpec((N_EXPERTS, D_MODEL), fix),
                  pl.BlockSpec((N_EXPERTS, 1), fix)],
        out_specs=[pl.BlockSpec((tm, D_MODEL), row), pl.BlockSpec((tm, 128), row),
                   pl.BlockSpec((4 * TOP_K, tm), lambda i: (0, i)), pl.BlockSpec((N_EXPERTS, 128), fix)],
        out_shape=[jax.ShapeDtypeStruct((t, D_MODEL), F32), jax.ShapeDtypeStruct((t, 128), F32),
                   jax.ShapeDtypeStruct((4 * TOP_K, t), F32), jax.ShapeDtypeStruct((N_EXPERTS, 128), F32)],
        scratch_shapes=[pltpu.VMEM((tm, tm), BF16)],
        compiler_params=_params(("arbitrary",)),
        name="out_router",
    )(oa, ob, oc, x2d, wo, g, beta, wr, br)


MOE_ROWS = 512
DISPATCH_TOKENS = 512


def _dispatch_body(d0_ref, d1_ref, d2_ref, d3_ref, pad_ref, x_ref, rows_out, zero_sc, sem, zero_sem):
    dest_refs = (d0_ref, d1_ref, d2_ref, d3_ref)
    @pl.when(pl.program_id(0) == 0)
    def _():
        zero_sc[...] = jnp.zeros_like(zero_sc)
        n_blocks = pad_ref.shape[0]

        def zero_copy(b):
            start = pl.multiple_of(b * MOE_ROWS, MOE_ROWS)
            return pltpu.make_async_copy(zero_sc, rows_out.at[pl.ds(start, MOE_ROWS), :], zero_sem)

        def zero_start(b, carry):
            @pl.when(pad_ref[b] > 0)
            def _():
                zero_copy(b).start()
            return carry

        def zero_wait(b, carry):
            @pl.when(pad_ref[b] > 0)
            def _():
                zero_copy(b).wait()
            return carry

        lax.fori_loop(0, n_blocks, zero_start, 0)
        lax.fori_loop(0, n_blocks, zero_wait, 0)

    def token_pair(r2, carry):
        for u in range(2):
            r = 2 * r2 + u
            for kk in range(TOP_K):
                d = dest_refs[kk][r]
                pltpu.make_async_copy(x_ref.at[pl.ds(r, 1), :], rows_out.at[pl.ds(d, 1), :],
                                      sem).start(priority=kk % 2)
        return carry

    lax.fori_loop(0, DISPATCH_TOKENS // 2, token_pair, 0)
    for _ in range(TOP_K):
        pltpu.make_async_copy(x_ref, x_ref, sem).wait()


def _dispatch(dest, block_padded, x1, n_rows):
    t = x1.shape[0]
    tm = DISPATCH_TOKENS
    slot_rows = pl.BlockSpec((tm,), lambda i: (i,), memory_space=pltpu.SMEM)
    return pl.pallas_call(
        _dispatch_body,
        grid=(t // tm,),
        in_specs=[slot_rows] * TOP_K
                 + [pl.BlockSpec(memory_space=pltpu.SMEM),
                    pl.BlockSpec((tm, D_MODEL), lambda i: (i, 0))],
        out_specs=pl.BlockSpec(memory_space=pl.ANY),
        out_shape=jax.ShapeDtypeStruct((n_rows, D_MODEL), F32),
        scratch_shapes=[pltpu.VMEM((MOE_ROWS, D_MODEL), F32), pltpu.SemaphoreType.DMA(()),
                        pltpu.SemaphoreType.DMA(())],
        compiler_params=_params(("arbitrary",)),
        name="moe_dispatch",
    )(*[dest[kk] for kk in range(TOP_K)], block_padded, x1)


def _moe_body(be_ref, bv_ref, src_ref, x_ref, wgu_ref, bgu_ref, wd_ref, bd_ref, y_ref, wgu_sc, wd_sc):
    del src_ref
    i = pl.program_id(0)
    prev = be_ref[jnp.maximum(i - 1, 0)]

    @pl.when((i == 0) | (be_ref[i] != prev))
    def _():
        wgu_sc[...] = wgu_ref[0, 0].astype(BF16)
        wd_sc[...] = wd_ref[0, 0].astype(BF16)

    @pl.when(bv_ref[i] > 0)
    def _():
        h = jnp.dot(x_ref[...].astype(BF16), wgu_sc[...], preferred_element_type=F32) + bgu_ref[0, 0]
        gate = jnp.minimum(h[:, :D_EXPERT], SWIGLU_LIMIT)
        up = jnp.clip(h[:, D_EXPERT:], -SWIGLU_LIMIT, SWIGLU_LIMIT)
        glu = gate * (1.0 / (1.0 + jnp.exp(-SWIGLU_ALPHA * gate)))
        act = ((up + 1.0) * glu).astype(BF16)
        y = jnp.dot(act, wd_sc[...], preferred_element_type=F32) + bd_ref[0, 0]
        y_ref[...] = _pack_bf16_pairs(y)

    @pl.when(bv_ref[i] == 0)
    def _():
        y_ref[...] = jnp.zeros_like(y_ref)


def _moe(layer, block_expert, block_valid, block_src, x_rows, wgu, bgu, wd, bd):
    n_rows = x_rows.shape[0]
    expert = lambda i, be, bv, src: (layer, be[i], 0, 0)
    grid_spec = pltpu.PrefetchScalarGridSpec(
        num_scalar_prefetch=3,
        grid=(n_rows // MOE_ROWS,),
        in_specs=[pl.BlockSpec((MOE_ROWS, D_MODEL), lambda i, be, bv, src: (src[i], 0)),
                  pl.BlockSpec((1, 1, D_MODEL, 2 * D_EXPERT), expert),
                  pl.BlockSpec((1, 1, 1, 2 * D_EXPERT), expert),
                  pl.BlockSpec((1, 1, D_EXPERT, D_MODEL), expert),
                  pl.BlockSpec((1, 1, 1, D_MODEL), expert)],
        out_specs=pl.BlockSpec((MOE_ROWS, D_MODEL // 2), lambda i, be, bv, src: (i, 0)),
        scratch_shapes=[pltpu.VMEM((D_MODEL, 2 * D_EXPERT), BF16), pltpu.VMEM((D_EXPERT, D_MODEL), BF16)],
    )
    return pl.pallas_call(
        _moe_body,
        grid_spec=grid_spec,
        out_shape=jax.ShapeDtypeStruct((n_rows, D_MODEL // 2), I32),
        compiler_params=_params(("arbitrary",)),
        name="moe_experts",
    )(block_expert, block_valid, block_src, x_rows, wgu, bgu, wd, bd)


def _combine_body(x_ref, y0_ref, y1_ref, y2_ref, y3_ref, meta_ref, g_ref, beta_ref, o_ref):
    gates = meta_ref[...]
    ffn = (gates[:, 8:9] * _unpack_bf16_pairs(y0_ref[0]) + gates[:, 9:10] * _unpack_bf16_pairs(y1_ref[0])
           + gates[:, 10:11] * _unpack_bf16_pairs(y2_ref[0]) + gates[:, 11:12] * _unpack_bf16_pairs(y3_ref[0]))
    o_ref[...] = _layer_norm(DEEPNORM_ALPHA * x_ref[...] + ffn, g_ref[...], beta_ref[...])


def _combine(x1, ys, meta, g, beta):
    t = x1.shape[0]
    tm = 512
    row = lambda i: (i, 0)
    fix = lambda i: (0, 0)
    slot = lambda kk: pl.BlockSpec((1, tm, D_MODEL // 2), lambda i: (kk, i, 0))
    return pl.pallas_call(
        _combine_body,
        grid=(t // tm,),
        in_specs=[pl.BlockSpec((tm, D_MODEL), row)] + [slot(kk) for kk in range(TOP_K)]
                 + [pl.BlockSpec((tm, 128), row), pl.BlockSpec((1, D_MODEL), fix),
                    pl.BlockSpec((1, D_MODEL), fix)],
        out_specs=pl.BlockSpec((tm, D_MODEL), row),
        out_shape=jax.ShapeDtypeStruct((t, D_MODEL), F32),
        compiler_params=_params(("parallel",)),
        name="combine_ln",
    )(x1, ys, ys, ys, ys, meta, g, beta)


def _prep_w_in(w):
    scale = DA ** -0.5 * LOG2E
    col = jnp.ones((D_QKV,), F32)
    for lo, width in ((0, HA * 2 * DA), (3 * HA * 2 * DA, HB * DB),
                      (3 * HA * 2 * DA + 3 * HB * DB, HC * DC)):
        col = col.at[lo:lo + width].set(scale)
    main = w[:, :D_QKV] * col
    q_idx = w[:, D_QKV:D_QKV + HI * DI]
    k_idx = w[:, D_QKV + HI * DI:D_QKV + HI * DI + DI]
    w_idx = w[:, D_QKV + HI * DI + DI:] * ((HI ** -0.5) * (DI ** -0.5))
    w_pad = jnp.zeros((D_MODEL, D_W - HI), F32)
    return jnp.concatenate([main, q_idx, jnp.tile(k_idx, (1, HI)), w_idx, w_pad], axis=1).astype(BF16)


def _route(slots, cnt, n_tok):
    top_e = slots[0:TOP_K].astype(I32)
    rank = slots[TOP_K:2 * TOP_K].astype(I32)
    counts = cnt[:, 0].astype(I32)
    n_blocks = n_tok * TOP_K // MOE_ROWS + N_EXPERTS
    blocks_e = (counts + MOE_ROWS - 1) // MOE_ROWS
    blocks_end = jnp.cumsum(blocks_e)
    row_start = (blocks_end - blocks_e) * MOE_ROWS
    start_of = jnp.zeros_like(top_e)
    for e in range(N_EXPERTS):
        start_of = jnp.where(top_e == e, row_start[e], start_of)
    dest = start_of + rank
    blk = jnp.arange(n_blocks, dtype=I32)
    block_expert = jnp.minimum(jnp.sum((blocks_end[None, :] <= blk[:, None]).astype(I32), axis=1),
                               N_EXPERTS - 1)
    block_valid = (blk < blocks_end[-1]).astype(I32)
    block_src = jnp.minimum(blk, blocks_end[-1] - 1)
    is_last = jnp.any((blocks_end[None, :] == blk[:, None] + 1) & (blocks_e[None, :] > 0), axis=1)
    block_padded = jnp.where(block_valid > 0, is_last, True).astype(I32)
    return dest, block_padded, n_blocks * MOE_ROWS, block_expert, block_valid, block_src


def kernel(x, w_in, lam_q1, lam_k1, lam_q2, lam_k2, subln_g, rel_bias, w_out, ln1_g, ln1_b,
           w_router, b_router, w_gu, b_gu, w_down, b_down, ln2_g, ln2_b):
    bsz, seq, _ = x.shape
    n_tok = bsz * seq
    slopes = [2.0 ** (-8.0 * i / (HA + HC)) for i in range(1, HA + HC + 1)]
    slopes_a = tuple(s * LOG2E for s in slopes[0::2])
    slopes_c = tuple(s * LOG2E for s in slopes[1::2])
    x2d = x.reshape(n_tok, D_MODEL)
    for l in range(DEPTH):
        lam_init = 0.8 - 0.6 * math.exp(-0.3 * l)
        lam = (jnp.exp(jnp.sum(lam_q1[l] * lam_k1[l])) - jnp.exp(jnp.sum(lam_q2[l] * lam_k2[l]))
               + lam_init).reshape(1).astype(F32)
        hm, hik, hw = _in_proj(x2d, _prep_w_in(w_in[l]))
        hm3 = hm.reshape(bsz, seq, D_QKV)
        out_a = _diff_attn(hm3, slopes_a, lam, subln_g[l].reshape(1, 2 * DA), lam_init)
        out_b = _band_attn(hm3, _band_bias(rel_bias[l]))
        out_c = _dsa_attn(hm3, hik.reshape(bsz, seq, D_IK), hw.reshape(bsz, seq, D_W), slopes_c)
        x1, meta, slots, cnt = _out_router(
            out_a.reshape(n_tok, -1), out_b.reshape(n_tok, -1), out_c.reshape(n_tok, -1), x2d,
            w_out[l].astype(BF16), ln1_g[l].reshape(1, -1), ln1_b[l].reshape(1, -1),
            w_router[l].T, b_router[l].reshape(-1, 1))
        dest, block_padded, n_rows, block_expert, block_valid, block_src = _route(slots, cnt, n_tok)
        x_rows = _dispatch(dest, block_padded, x1, n_rows)
        y_rows = _moe(l, block_expert, block_valid, block_src, x_rows, w_gu,
                      b_gu.reshape(DEPTH, N_EXPERTS, 1, -1), w_down, b_down.reshape(DEPTH, N_EXPERTS, 1, -1))
        ys = y_rows[dest]
        x2d = _combine(x1, ys, meta, ln2_g[l].reshape(1, -1), ln2_b[l].reshape(1, -1))
    return x2d.reshape(bsz, seq, D_MODEL)
```
